```python
import math
import jax, jax.numpy as jnp
from jax import lax
import numpy as np

D_MODEL = 1024
BATCH = 16
SEQ = 4096
DEPTH = 4

CTX_LEN = 256
GRID_W = 64
A_HEADS = 4
A_HD = 64
A_VD = 2 * A_HD
A_QK = A_HEADS * 2 * A_HD
A_VW = A_HEADS * A_VD
M_HEADS = 4
M_QK = 64
M_V = 128
M_QKW = M_HEADS * M_QK
M_VW = M_HEADS * M_V
G_HEADS = 4
G_HD = 128
G_W = G_HEADS * G_HD
CONV_K = 5
BRANCH_W = 512
D_FF = 4 * D_MODEL
CHUNK = 64
Q_BLOCK = 128
ROPE_BASE = 10000.0
EPS = 1e-6
IN_SPLITS = (A_QK, A_QK, A_VW,
             M_QKW, M_QKW, M_VW, M_VW, 4 * M_HEADS,
             G_W, G_W, G_W, G_W, 4 * G_HEADS,
             3 * D_MODEL)
D_IN = sum(IN_SPLITS)

kernel_name = 'hybrid_diffattn_mlstm_gdn_dit_trunk'


def _rmsnorm(x, gain):
    x32 = x.astype(jnp.float32)
    y = x32 * lax.rsqrt(jnp.mean(x32 * x32, axis=-1, keepdims=True) + EPS)
    return (y * gain.astype(jnp.float32)).astype(x.dtype)


def _l2norm(x):
    x32 = x.astype(jnp.float32)
    return (x32 * lax.rsqrt(jnp.sum(x32 * x32, axis=-1, keepdims=True) + EPS)).astype(x.dtype)


def _heads(t, h):
    B, T, _ = t.shape
    return jnp.moveaxis(t.reshape(B, T, h, -1), 2, 1)


def _merge_heads(t):
    B, H, T, d = t.shape
    return jnp.moveaxis(t, 1, 2).reshape(B, T, H * d)


def _flip(t):
    return jnp.flip(t, axis=2)


def _grid_rope_tables(n_lat):
    rows = n_lat // GRID_W
    row = jnp.repeat(jnp.arange(rows, dtype=jnp.float32), GRID_W)
    col = jnp.tile(jnp.arange(GRID_W, dtype=jnp.float32), rows)
    half = A_HD // 2
    inv_freq = ROPE_BASE ** (-jnp.arange(0, half, 2, dtype=jnp.float32) / half)
    ang_r = row[:, None] * inv_freq
    ang_c = col[:, None] * inv_freq
    ang = jnp.concatenate([ang_r, ang_r, ang_c, ang_c], axis=-1)
    return jnp.cos(ang), jnp.sin(ang)


def _apply_rope(x, cos, sin):
    x32 = x.astype(jnp.float32)
    a1, a2, b1, b2 = jnp.split(x32, 4, axis=-1)
    rot = jnp.concatenate([-a2, a1, -b2, b1], axis=-1)
    c = cos[None, :, None, None, :]
    s = sin[None, :, None, None, :]
    return (x32 * c + rot * s).astype(x.dtype)


def _centred_conv(x, w):
    return lax.conv_general_dilated(
        x, w[:, None, :].astype(x.dtype), window_strides=(1,),
        padding=((CONV_K // 2, CONV_K // 2),),
        dimension_numbers=('NWC', 'WIO', 'NWC'),
        feature_group_count=x.shape[-1])


def _diff_attend(q, k, v, lam):
    s = jnp.einsum('bqhmd,bkhmd->bhmqk', q, k).astype(jnp.float32) * (A_HD ** -0.5)
    p = jax.nn.softmax(s, axis=-1)
    w = p[:, :, 0] - lam * p[:, :, 1]
    return jnp.einsum('bhqk,bkhe->bqhe', w.astype(v.dtype), v)


def _diff_attn_branch(lat, ctx, lam_vec, lam_init, gain, cos, sin, need_ctx_out):
    q, k, v = lat
    qc, kc, vc = ctx
    B, n, _ = q.shape
    nc = qc.shape[1]
    q5 = _apply_rope(q.reshape(B, n, A_HEADS, 2, A_HD), cos, sin)
    k5 = _apply_rope(k.reshape(B, n, A_HEADS, 2, A_HD), cos, sin)
    kc5 = kc.reshape(B, nc, A_HEADS, 2, A_HD)
    vc4 = vc.reshape(B, nc, A_HEADS, A_VD)
    lam_vec = lam_vec.astype(jnp.float32)
    lam = (jnp.exp(jnp.sum(lam_vec[0] * lam_vec[1])) -
           jnp.exp(jnp.sum(lam_vec[2] * lam_vec[3])) + lam_init)
    k_all = jnp.concatenate([k5, kc5], axis=1)
    v_all = jnp.concatenate([v.reshape(B, n, A_HEADS, A_VD), vc4], axis=1)
    nb = n // Q_BLOCK
    qb = jnp.moveaxis(q5.reshape(B, nb, Q_BLOCK, A_HEADS, 2, A_HD), 1, 0)
    o = lax.map(lambda blk: _diff_attend(blk, k_all, v_all, lam), qb)
    o = jnp.moveaxis(o, 0, 1).reshape(B, n, A_HEADS, A_VD)

    def post(t):
        return (_rmsnorm(t, gain) * (1.0 - lam_init)).reshape(t.shape[0], t.shape[1], A_VW)

    y = post(o)
    yc = post(_diff_attend(qc.reshape(B, nc, A_HEADS, 2, A_HD), kc5, vc4, lam)) if need_ctx_out else None
    return y, yc


def _to_chunks(t, n_chunks):
    shp = t.shape
    t = t.reshape(shp[0], shp[1], n_chunks, CHUNK, *shp[3:])
    return jnp.moveaxis(t, 2, 0)


def _from_chunks(t):
    t = jnp.moveaxis(t, 0, 2)
    return t.reshape(t.shape[0], t.shape[1], -1, *t.shape[4:])


def _mlstm_scan(q, k, v, log_i, log_f, state, with_output):
    f32 = jnp.float32
    n_chunks = q.shape[2] // CHUNK
    xs = tuple(_to_chunks(t.astype(f32), n_chunks)
               for t in (q, k.astype(f32) * (M_QK ** -0.5), v, log_i, log_f))
    causal = jnp.tril(jnp.ones((CHUNK, CHUNK), dtype=bool))

    def body(carry, blk):
        C, nv, m = carry
        qc, kc, vc, ic, fc = blk
        b = jnp.cumsum(fc, axis=-1)
        b_end = b[..., -1]
        a = b_end[..., None] - b + ic
        m_new = jnp.maximum(b_end + m, jnp.max(a, axis=-1))
        carry_w = jnp.exp(b_end + m - m_new)
        key_w = jnp.exp(a - m_new[..., None])
        C_new = carry_w[..., None, None] * C + jnp.einsum('bhs,bhsk,bhsv->bhkv', key_w, kc, vc)
        n_new = carry_w[..., None] * nv + jnp.einsum('bhs,bhsk->bhk', key_w, kc)
        if not with_output:
            return (C_new, n_new, m_new), None
        d = jnp.where(causal, b[..., :, None] - b[..., None, :] + ic[..., None, :], -jnp.inf)
        inter = b + m[..., None]
        m_row = jnp.maximum(inter, jnp.max(d, axis=-1))
        w = jnp.exp(d - m_row[..., None])
        e_inter = jnp.exp(inter - m_row)
        s = jnp.einsum('bhjk,bhsk->bhjs', qc, kc) * w
        num = (jnp.einsum('bhjs,bhsv->bhjv', s, vc) +
               e_inter[..., None] * jnp.einsum('bhjk,bhkv->bhjv', qc, C))
        den = jnp.sum(s, axis=-1) + e_inter * jnp.einsum('bhjk,bhk->bhj', qc, nv)
        h = num / jnp.maximum(jnp.abs(den), jnp.exp(-m_row))[..., None]
        return (C_new, n_new, m_new), h

    state, h = lax.scan(body, state, xs)
    out = _from_chunks(h).astype(v.dtype) if with_output else None
    return out, state


def _mlstm_branch(lat, ctx, f_bias, gain, need_ctx_out):
    f32 = jnp.float32

    def prep(q, k, v, o, gates):
        gi_f, gf_f, gi_b, gf_b = [jnp.moveaxis(t, 2, 1) for t in jnp.split(gates.astype(f32), 4, axis=-1)]
        lf_f = jax.nn.log_sigmoid(gf_f + f_bias[0][:, None])
        lf_b = jax.nn.log_sigmoid(gf_b + f_bias[1][:, None])
        return _heads(q, M_HEADS), _heads(k, M_HEADS), _heads(v, M_HEADS), gi_f, lf_f, gi_b, lf_b, o

    def post(h_f, h_b_rev, o):
        h = _rmsnorm(h_f + _flip(h_b_rev), gain[:, None, :])
        return _merge_heads(h) * jax.nn.sigmoid(o)

    B = lat[0].shape[0]
    zero = (jnp.zeros((B, M_HEADS, M_QK, M_V), f32), jnp.zeros((B, M_HEADS, M_QK), f32),
            jnp.zeros((B, M_HEADS), f32))
    qc, kc, vc, icf, lfcf, icb, lfcb, oc = prep(*ctx)
    hc_f, st_f = _mlstm_scan(qc, kc, vc, icf, lfcf, zero, need_ctx_out)
    hc_b, st_b = _mlstm_scan(_flip(qc), _flip(kc), _flip(vc), _flip(icb), _flip(lfcb), zero, need_ctx_out)
    q, k, v, i_f, lf_f, i_b, lf_b, o = prep(*lat)
    h_f, _ = _mlstm_scan(q, k, v, i_f, lf_f, st_f, True)
    h_b, _ = _mlstm_scan(_flip(q), _flip(k), _flip(v), _flip(i_b), _flip(lf_b), st_b, True)
    y = post(h_f, h_b, o)
    yc = post(hc_f, hc_b, oc) if need_ctx_out else None
    return y, yc


def _gdn_scan(q, k, v, g, beta, S, with_output):
    f32 = jnp.float32
    n_chunks = q.shape[2] // CHUNK

    def chunk(t):
        shp = t.shape
        return t.astype(f32).reshape(shp[0], shp[1], n_chunks, CHUNK, *shp[3:])

    qc, kc, vc, gcn, bc = [chunk(t) for t in (q, k, v, g, beta)]
    gcum = jnp.cumsum(gcn, axis=-1)
    incl = jnp.tril(jnp.ones((CHUNK, CHUNK), dtype=bool))
    strict = jnp.tril(jnp.ones((CHUNK, CHUNK), dtype=bool), -1)
    diff = gcum[..., :, None] - gcum[..., None, :]
    decay = jnp.where(incl, jnp.exp(jnp.where(incl, diff, 0.0)), 0.0)
    kb = kc * bc[..., None]
    A = jnp.where(strict, jnp.einsum('bhnid,bhnjd->bhnij', kb, kc) * decay, 0.0)
    eye = jnp.eye(CHUNK, dtype=f32)
    T = lax.linalg.triangular_solve(A + eye, jnp.broadcast_to(eye, A.shape),
                                    left_side=True, lower=True, unit_diagonal=True)
    u = jnp.einsum('bhnij,bhnjv->bhniv', T, vc * bc[..., None])
    w = jnp.einsum('bhnij,bhnjk->bhnik', T, kb * jnp.exp(gcum)[..., None])
    g_end = gcum[..., -1]
    k_end = kc * jnp.exp(g_end[..., None] - gcum)[..., None]
    xs = [u, w, k_end, g_end]
    if with_output:
        qk = jnp.where(incl, jnp.einsum('bhnid,bhnjd->bhnij', qc, kc) * decay, 0.0)
        xs += [qc * jnp.exp(gcum)[..., None], qk]
    xs = tuple(jnp.moveaxis(t, 2, 0) for t in xs)

    def body(S, blk):
        u_c, w_c, ke_c, ge_c = blk[:4]
        v_new = u_c - jnp.einsum('bhlk,bhkv->bhlv', w_c, S)
        S_new = S * jnp.exp(ge_c)[..., None, None] + jnp.einsum('bhlk,bhlv->bhkv', ke_c, v_new)
        if not with_output:
            return S_new, None
        q_c, qk_c = blk[4:]
        o = jnp.einsum('bhlk,bhkv->bhlv', q_c, S) + jnp.einsum('bhij,bhjv->bhiv', qk_c, v_new)
        return S_new, o

    S, o = lax.scan(body, S.astype(f32), xs)
    out = _from_chunks(o).astype(v.dtype) if with_output else None
    return out, S


def _gdn_branch(lat, ctx, conv_w, a_log, dt_bias, gain, need_ctx_out):
    f32 = jnp.float32

    def prep(q, k, v, z, gates):
        qkv = jax.nn.silu(_centred_conv(jnp.concatenate([q, k, v], axis=-1), conv_w))
        q, k, v = jnp.split(qkv, 3, axis=-1)
        q = _l2norm(_heads(q, G_HEADS)) * (G_HD ** -0.5)
        k = _l2norm(_heads(k, G_HEADS))
        v = _heads(v, G_HEADS)
        a_f, a_b, b_f, b_b = [jnp.moveaxis(t, 2, 1) for t in jnp.split(gates.astype(f32), 4, axis=-1)]
        g_f = -jnp.exp(a_log[0])[:, None] * jax.nn.softplus(a_f + dt_bias[0][:, None])
        g_b = -jnp.exp(a_log[1])[:, None] * jax.nn.softplus(a_b + dt_bias[1][:, None])
        return q, k, v, g_f, jax.nn.sigmoid(b_f), g_b, jax.nn.sigmoid(b_b), z

    def post(o_f, o_b_rev, z):
        o = _rmsnorm(o_f + _flip(o_b_rev), gain)
        return _merge_heads(o) * jax.nn.silu(z)

    B = lat[0].shape[0]
    zero = jnp.zeros((B, G_HEADS, G_HD, G_HD), f32)
    qc, kc, vc, gcf, bcf, gcb, bcb, zc = prep(*ctx)
    oc_f, st_f = _gdn_scan(qc, kc, vc, gcf, bcf, zero, need_ctx_out)
    oc_b, st_b = _gdn_scan(_flip(qc), _flip(kc), _flip(vc), _flip(gcb), _flip(bcb), zero, need_ctx_out)
    q, k, v, g_f, b_f, g_b, b_b, z = prep(*lat)
    o_f, _ = _gdn_scan(q, k, v, g_f, b_f, st_f, True)
    o_b, _ = _gdn_scan(_flip(q), _flip(k), _flip(v), _flip(g_b), _flip(b_b), st_b, True)
    y = post(o_f, o_b, z)
    yc = post(oc_f, oc_b, zc) if need_ctx_out else None
    return y, yc


def _merge_branches(ya, ym, yg, gates, w_branch, w_out):
    ga, gm, gg = jnp.split(jax.nn.sigmoid(gates), 3, axis=-1)
    y = ga * (ya @ w_branch[0]) + gm * (ym @ w_branch[1]) + gg * (yg @ w_branch[2])
    return y @ w_out


def _sq_relu_mlp(h, w1, w2):
    return jnp.square(jax.nn.relu(h @ w1)) @ w2


def _layer(x, cx, c_silu, cctx_silu, w_ada, b_ada, norm_gains, w_in, b_in, a_lambda, a_norm,
           m_fbias, m_norm, g_conv, g_alog, g_dtbias, g_norm, w_branch, w_out, w_ff1, w_ff2,
           lam_init, cos, sin, need_ctx_out):
    sh1, sc1, gt1, sh2, sc2, gt2 = [t[:, None, :] for t in jnp.split(c_silu @ w_ada + b_ada, 6, axis=-1)]
    sh1c, sc1c, gt1c, sh2c, sc2c, gt2c = jnp.split(cctx_silu @ w_ada + b_ada, 6, axis=-1)
    split_idx = np.cumsum(IN_SPLITS)[:-1].tolist()

    h = _rmsnorm(x, norm_gains[0]) * (1.0 + sc1) + sh1
    hc = _rmsnorm(cx, norm_gains[0]) * (1.0 + sc1c) + sh1c
    p = jnp.split(h @ w_in + b_in, split_idx, axis=-1)
    pc = jnp.split(hc @ w_in + b_in, split_idx, axis=-1)
    ya, yac = _diff_attn_branch(p[0:3], pc[0:3], a_lambda, lam_init, a_norm, cos, sin, need_ctx_out)
    ym, ymc = _mlstm_branch(p[3:8], pc[3:8], m_fbias, m_norm, need_ctx_out)
    yg, ygc = _gdn_branch(p[8:13], pc[8:13], g_conv, g_alog, g_dtbias, g_norm, need_ctx_out)
    x = x + gt1 * _rmsnorm(_merge_branches(ya, ym, yg, p[13], w_branch, w_out), norm_gains[1])

    h2 = _rmsnorm(x, norm_gains[2]) * (1.0 + sc2) + sh2
    x = x + gt2 * _rmsnorm(_sq_relu_mlp(h2, w_ff1, w_ff2), norm_gains[3])

    if need_ctx_out:
        cx = cx + gt1c * _rmsnorm(_merge_branches(yac, ymc, ygc, pc[13], w_branch, w_out), norm_gains[1])
        h2c = _rmsnorm(cx, norm_gains[2]) * (1.0 + sc2c) + sh2c
        cx = cx + gt2c * _rmsnorm(_sq_relu_mlp(h2c, w_ff1, w_ff2), norm_gains[3])
    return x, cx


def setup_inputs(seed: int = 0) -> dict:
    key = jax.random.key(seed)
    ks = jax.random.split(key, 24)
    f32 = jnp.float32

    def nrm(k, shape, scale):
        return jax.random.normal(k, shape, f32) * scale

    dt = jnp.exp(jax.random.uniform(ks[15], (DEPTH, 2, G_HEADS), f32, math.log(1e-3), math.log(1e-1)))
    return {
        'x': nrm(ks[0], (BATCH, SEQ, D_MODEL), 1.0),
        'c': nrm(ks[1], (BATCH, D_MODEL), 1.0),
        'ctx': nrm(ks[2], (BATCH, CTX_LEN, D_MODEL), 1.0),
        'c_ctx': nrm(ks[3], (D_MODEL,), 1.0),
        'w_ada': nrm(ks[4], (DEPTH, D_MODEL, 6 * D_MODEL), 0.5 * D_MODEL ** -0.5),
        'b_ada': nrm(ks[5], (DEPTH, 6 * D_MODEL), 0.02),
        'norm_gains': 1.0 + nrm(ks[6], (DEPTH, 4, D_MODEL), 0.02),
        'w_in': nrm(ks[7], (DEPTH, D_MODEL, D_IN), D_MODEL ** -0.5),
        'b_in': nrm(ks[8], (DEPTH, D_IN), 0.02),
        'a_lambda': nrm(ks[9], (DEPTH, 4, A_HD), 0.1),
        'a_norm': 1.0 + nrm(ks[10], (DEPTH, A_VD), 0.02),
        'm_fbias': jnp.linspace(3.0, 6.0, M_HEADS, dtype=f32)[None, None, :] + nrm(ks[11], (DEPTH, 2, M_HEADS), 0.1),
        'm_norm': 1.0 + nrm(ks[12], (DEPTH, M_HEADS, M_V), 0.02),
        'g_conv': nrm(ks[13], (DEPTH, CONV_K, 3 * G_W), CONV_K ** -0.5),
        'g_alog': jnp.log(jax.random.uniform(ks[14], (DEPTH, 2, G_HEADS), f32, 1.0, 16.0)),
        'g_dtbias': dt + jnp.log(-jnp.expm1(-dt)),
        'g_norm': 1.0 + nrm(ks[16], (DEPTH, G_HD), 0.02),
        'w_branch': nrm(ks[17], (DEPTH, 3, BRANCH_W, D_MODEL), BRANCH_W ** -0.5),
        'w_out': nrm(ks[18], (DEPTH, D_MODEL, D_MODEL), D_MODEL ** -0.5),
        'w_ff1': nrm(ks[19], (DEPTH, D_MODEL, D_FF), D_MODEL ** -0.5),
        'w_ff2': nrm(ks[20], (DEPTH, D_FF, D_MODEL), D_FF ** -0.5),
    }


def reference(x, c, ctx, c_ctx, w_ada, b_ada, norm_gains, w_in, b_in, a_lambda, a_norm,
              m_fbias, m_norm, g_conv, g_alog, g_dtbias, g_norm, w_branch, w_out, w_ff1, w_ff2):
    cos, sin = _grid_rope_tables(x.shape[1])
    c_silu = jax.nn.silu(c)
    cctx_silu = jax.nn.silu(c_ctx)
    cx = ctx
    for i in range(DEPTH):
        lam_init = 0.8 - 0.6 * math.exp(-0.3 * i)
        x, cx = _layer(x, cx, c_silu, cctx_silu, w_ada[i], b_ada[i], norm_gains[i], w_in[i], b_in[i],
                       a_lambda[i], a_norm[i], m_fbias[i], m_norm[i], g_conv[i], g_alog[i],
                       g_dtbias[i], g_norm[i], w_branch[i], w_out[i], w_ff1[i], w_ff2[i],
                       lam_init, cos, sin, i < DEPTH - 1)
    return x
```

```python
import functools
import math

import jax
import jax.numpy as jnp
import numpy as np
from jax import lax
from jax.experimental import pallas as pl
from jax.experimental.pallas import tpu as pltpu

F32 = jnp.float32
BF16 = jnp.bfloat16

D_MODEL = 1024
GRID_W = 64
A_HEADS = 4
A_HD = 64
A_VD = 2 * A_HD
A_QK = A_HEADS * 2 * A_HD
A_VW = A_HEADS * A_VD
M_HEADS = 4
M_QK = 64
M_V = 128
M_QKW = M_HEADS * M_QK
M_VW = M_HEADS * M_V
G_HEADS = 4
G_HD = 128
G_W = G_HEADS * G_HD
CONV_K = 5
BRANCH_W = 512
D_FF = 4 * D_MODEL
ROPE_BASE = 10000.0
EPS = 1e-6
N_GATES = 16

LANES = 128
TOKEN_TILE = 256
KEY_TILE = 256
CHUNK = 64
VMEM_LIMIT = 56 * 1024 * 1024

_OFF = np.cumsum([0, A_QK, A_QK, A_VW, M_QKW, M_QKW, M_VW, M_VW, 4 * M_HEADS,
                  G_W, G_W, G_W, G_W, 4 * G_HEADS, 3 * D_MODEL])
(_O_AQ, _O_AK, _O_AV, _O_MQ, _O_MK, _O_MV, _O_MO, _O_MG,
 _O_GQ, _O_GK, _O_GV, _O_GZ, _O_GG, _O_MRG, _O_END) = [int(v) for v in _OFF]

PA_W = 3 * A_QK
PM_W = M_QKW * 2 + 2 * M_VW + LANES
PM_GATE_BLK = (M_QKW * 2 + 2 * M_VW) // LANES
PG_W = 4 * G_W + LANES
PG_GATE_BLK = 4 * G_W // LANES


def _params(sem):
    return pltpu.CompilerParams(dimension_semantics=sem, vmem_limit_bytes=VMEM_LIMIT)


def _sigmoid(x):
    return 1.0 / (1.0 + jnp.exp(-x))


def _softplus(x):
    return jnp.maximum(x, 0.0) + jnp.log1p(jnp.exp(-jnp.abs(x)))


def _log_sigmoid(x):
    return -_softplus(-x)


def _rms(x, gain):
    return x * lax.rsqrt(jnp.mean(x * x, axis=-1, keepdims=True) + EPS) * gain


def _dot(a, b):
    return jnp.dot(a.astype(BF16), b.astype(BF16), preferred_element_type=F32)


def _dot_nt(a, b):
    return lax.dot_general(a.astype(BF16), b.astype(BF16), (((1,), (1,)), ((), ())),
                           preferred_element_type=F32)


def _dot_tn(a, b):
    return lax.dot_general(a.astype(BF16), b.astype(BF16), (((0,), (0,)), ((), ())),
                           preferred_element_type=F32)


def _dot_exact(a, b):
    return jnp.dot(a, b, precision=lax.Precision.HIGHEST, preferred_element_type=F32)


def _ada_kernel(c_ref, w_ref, b_ref, o_ref):
    c = c_ref[...]
    s = c * _sigmoid(c)
    o_ref[...] = _dot_exact(s, w_ref[...]) + b_ref[...]


def _ada_all(cond, w_ada, b_ada):
    depth = w_ada.shape[0]
    rows = cond.shape[0]
    n = w_ada.shape[2]
    tn = 1024
    return pl.pallas_call(
        _ada_kernel,
        grid=(depth, n // tn),
        in_specs=[pl.BlockSpec((rows, D_MODEL), lambda l, j: (0, 0)),
                  pl.BlockSpec((None, D_MODEL, tn), lambda l, j: (l, 0, j)),
                  pl.BlockSpec((None, 1, tn), lambda l, j: (l, 0, j))],
        out_specs=pl.BlockSpec((None, rows, tn), lambda l, j: (l, 0, j)),
        out_shape=jax.ShapeDtypeStruct((depth, rows, n), F32),
        compiler_params=_params(("parallel", "parallel")),
        name="ada",
    )(cond, w_ada, b_ada.reshape(depth, 1, n))


def _rot_half(x):
    lane = lax.broadcasted_iota(jnp.int32, x.shape, 1)
    first = (lane % 32) < 16
    return jnp.where(first, -pltpu.roll(x, LANES - 16, 1), pltpu.roll(x, 16, 1))


def _proj_kernel(x_ref, mod_ref, g_ref, w_ref, b_ref, *rest, epilogue, nct, t_off):
    x = x_ref[...]
    h = _rms(x, g_ref[...]) * (1.0 + mod_ref[1:2, :]) + mod_ref[0:1, :]
    acc = jnp.dot(h.astype(BF16), w_ref[...], preferred_element_type=F32) + b_ref[...]
    if epilogue == "attn":
        cos_ref, sin_ref, o_ref = rest
        is_lat = (pl.program_id(1) + t_off) >= nct
        cos = jnp.where(is_lat, cos_ref[...], 1.0)
        sin = jnp.where(is_lat, sin_ref[...], 0.0)
        for j in range(2 * A_QK // LANES):
            blk = acc[:, j * LANES:(j + 1) * LANES]
            r = blk * cos + _rot_half(blk) * sin
            if j < A_QK // LANES:
                r = r * (A_HD ** -0.5)
            o_ref[:, j * LANES:(j + 1) * LANES] = r.astype(o_ref.dtype)
        o_ref[:, 2 * A_QK:] = acc[:, 2 * A_QK:].astype(o_ref.dtype)
    elif epilogue == "sigmoid":
        (o_ref,) = rest
        o_ref[...] = _sigmoid(acc).astype(o_ref.dtype)
    else:
        (o_ref,) = rest
        o_ref[...] = acc.astype(o_ref.dtype)


def _proj(x_all, mod, gain, w, b, *, epilogue, out_dtype, nct, rope=None):
    bsz, t, _ = x_all.shape
    n = w.shape[1]
    nt = t // TOKEN_TILE
    in_specs = [
        pl.BlockSpec((None, TOKEN_TILE, D_MODEL), lambda i, j: (i, j, 0)),
        pl.BlockSpec((None, 6, D_MODEL), lambda i, j: (2 * i + (j >= nct).astype(jnp.int32), 0, 0)),
        pl.BlockSpec((1, D_MODEL), lambda i, j: (0, 0)),
        pl.BlockSpec((D_MODEL, n), lambda i, j: (0, 0)),
        pl.BlockSpec((1, n), lambda i, j: (0, 0)),
    ]
    args = [x_all, mod, gain, w, b]
    if rope is not None:
        spec = pl.BlockSpec((TOKEN_TILE, LANES), lambda i, j: (jnp.maximum(j - nct, 0), 0))
        in_specs += [spec, spec]
        args += list(rope)
    return pl.pallas_call(
        functools.partial(_proj_kernel, epilogue=epilogue, nct=nct, t_off=0),
        grid=(bsz, nt),
        in_specs=in_specs,
        out_specs=pl.BlockSpec((None, TOKEN_TILE, n), lambda i, j: (i, j, 0)),
        out_shape=jax.ShapeDtypeStruct((bsz, t, n), out_dtype),
        compiler_params=_params(("parallel", "parallel")),
        name="proj_" + epilogue,
    )(*args)


def _attn_kernel(q_ref, k_ref, v_ref, lam_ref, an_ref, o_ref, *, nct, tc, t_all, lam_init):
    qi = pl.program_id(2)
    q = q_ref[...]
    lane = lax.broadcasted_iota(jnp.int32, q.shape, 1)
    zero = jnp.zeros_like(q)
    qm = (jnp.where(lane < A_HD, q, zero), jnp.where(lane >= A_HD, q, zero))
    lv = lam_ref[...].astype(F32)
    lam = (jnp.exp(jnp.sum(lv[0:1] * lv[1:2], axis=-1, keepdims=True))
           - jnp.exp(jnp.sum(lv[2:3] * lv[3:4], axis=-1, keepdims=True)) + lam_init)
    nk = jnp.where(qi < nct, tc // KEY_TILE, t_all // KEY_TILE)
    tq = q.shape[0]

    def body(j, carry):
        start = pl.multiple_of(j * KEY_TILE, KEY_TILE)
        ks = k_ref[pl.ds(start, KEY_TILE), :]
        vs = v_ref[pl.ds(start, KEY_TILE), :]
        out = []
        for mp in range(2):
            m_prev, l_prev, acc_prev = carry[3 * mp:3 * mp + 3]
            s = lax.dot_general(qm[mp], ks, (((1,), (1,)), ((), ())), preferred_element_type=F32)
            m_new = jnp.maximum(m_prev, jnp.max(s, axis=-1, keepdims=True))
            alpha = jnp.exp(m_prev - m_new)
            p = jnp.exp(s - m_new)
            l_new = alpha * l_prev + jnp.sum(p, axis=-1, keepdims=True)
            acc_new = alpha * acc_prev + jnp.dot(p.astype(BF16), vs, preferred_element_type=F32)
            out += [m_new, l_new, acc_new]
        return tuple(out)

    init = (jnp.full((tq, 1), -jnp.inf, F32), jnp.zeros((tq, 1), F32), jnp.zeros((tq, A_VD), F32)) * 2
    m0, l0, a0, m1, l1, a1 = lax.fori_loop(0, nk, body, init)
    o = a0 / l0 - lam * (a1 / l1)
    o_ref[...] = _rms(o, an_ref[...]) * (1.0 - lam_init)


def _attention(pa, a_lambda, a_norm, *, nct, tc, lam_init):
    bsz, t, _ = pa.shape
    nt = t // TOKEN_TILE
    kb = A_QK // LANES
    return pl.pallas_call(
        functools.partial(_attn_kernel, nct=nct, tc=tc, t_all=t, lam_init=lam_init),
        grid=(bsz, A_HEADS, nt),
        in_specs=[
            pl.BlockSpec((None, TOKEN_TILE, LANES), lambda b, h, i: (b, i, h)),
            pl.BlockSpec((None, t, LANES), lambda b, h, i: (b, 0, kb + h)),
            pl.BlockSpec((None, t, LANES), lambda b, h, i: (b, 0, 2 * kb + h)),
            pl.BlockSpec((4, A_HD), lambda b, h, i: (0, 0)),
            pl.BlockSpec((1, A_VD), lambda b, h, i: (0, 0)),
        ],
        out_specs=pl.BlockSpec((None, TOKEN_TILE, LANES), lambda b, h, i: (b, i, h)),
        out_shape=jax.ShapeDtypeStruct((bsz, t, A_VW), F32),
        compiler_params=_params(("parallel", "parallel", "arbitrary")),
        name="diff_attn",
    )(pa, pa, pa, a_lambda, a_norm.reshape(1, A_VD))


def _conv_kernel(x_ref, w_ref, o_ref, *, tc):
    j = pl.program_id(1)
    x = x_ref[...]
    t_all = x.shape[0]
    w = w_ref[...]
    row = lax.broadcasted_iota(jnp.int32, x.shape, 0)
    in_ctx = row < tc
    pos = jnp.where(in_ctx, row, row - tc)
    seg = jnp.where(in_ctx, tc, t_all - tc)
    half = CONV_K // 2
    acc = x * w[half:half + 1, :]
    for d in range(-half, half + 1):
        if d == 0:
            continue
        shifted = pltpu.roll(x, (-d) % t_all, 0)
        ok = (pos + d >= 0) & (pos + d < seg)
        acc = acc + jnp.where(ok, shifted, 0.0) * w[half + d:half + d + 1, :]
    y = acc * _sigmoid(acc)
    nrm = y * lax.rsqrt(jnp.sum(y * y, axis=-1, keepdims=True) + EPS)
    nq = G_W // LANES
    o_ref[...] = jnp.where(j < nq, nrm * (G_HD ** -0.5), jnp.where(j < 2 * nq, nrm, y))


def _gdn_conv(pg, conv_w, *, tc):
    bsz, t, _ = pg.shape
    nblk = 3 * G_W // LANES
    return pl.pallas_call(
        functools.partial(_conv_kernel, tc=tc),
        grid=(bsz, nblk),
        in_specs=[pl.BlockSpec((None, t, LANES), lambda b, j: (b, 0, j)),
                  pl.BlockSpec((CONV_K, LANES), lambda b, j: (0, j))],
        out_specs=pl.BlockSpec((None, t, LANES), lambda b, j: (b, 0, j)),
        out_shape=jax.ShapeDtypeStruct((bsz, t, 3 * G_W), F32),
        compiler_params=_params(("parallel", "parallel")),
        name="gdn_conv",
    )(pg, conv_w)


def _tri_masks(n):
    row = lax.broadcasted_iota(jnp.int32, (n, n), 0)
    col = lax.broadcasted_iota(jnp.int32, (n, n), 1)
    return col <= row, col >= row, col < row, col > row


def _bwd_tile(s, nct, nt):
    return jnp.where(s < nct, nct - 1 - s, nt - 1 - (s - nct))


def _chunk_gates_t(gates):
    bsz, t, n = gates.shape
    return jnp.swapaxes(gates.reshape(bsz, t // CHUNK, CHUNK, n), 2, 3)


def _mlstm_chunk(q, k, v, i_c, b_c, i_r, b_r, btot, c_st, n_st, m_st, incl):
    a_r = btot - b_r + i_r
    a_c = btot - b_c + i_c
    m_new = jnp.maximum(btot + m_st, jnp.max(a_r, axis=-1, keepdims=True))
    carry_w = jnp.exp(btot + m_st - m_new)
    kw = k * jnp.exp(a_c - m_new)
    c_new = carry_w * c_st + _dot_tn(kw, v)
    n_new = carry_w * n_st + jnp.sum(kw, axis=0, keepdims=True)

    d = jnp.where(incl, b_c - b_r + i_r, -jnp.inf)
    inter = b_c + m_st
    m_row = jnp.maximum(inter, jnp.max(d, axis=-1, keepdims=True))
    w = jnp.exp(d - m_row)
    e_inter = jnp.exp(inter - m_row)
    s = _dot_nt(q, k) * w
    num = _dot(s, v) + e_inter * _dot(q, c_st)
    den = jnp.sum(s, axis=-1, keepdims=True) + e_inter * jnp.sum(q * n_st, axis=-1, keepdims=True)
    h = num / jnp.maximum(jnp.abs(den), jnp.exp(-m_row))
    return h, c_new, n_new, m_new


def _mlstm_kernel(qkv_f, g_f, gt_f, qkv_b, g_b, gt_b, fbr_ref, fbc_ref, hf_ref, hb_ref,
                  c_scr, n_scr, m_scr):
    @pl.when(pl.program_id(1) == 0)
    def _():
        c_scr[...] = jnp.zeros_like(c_scr)
        n_scr[...] = jnp.zeros_like(n_scr)
        m_scr[...] = jnp.zeros_like(m_scr)

    lower, upper, _, _ = _tri_masks(CHUNK)
    lower_f, upper_f = lower.astype(F32), upper.astype(F32)
    fbr = fbr_ref[...]
    fbc = fbc_ref[...]
    nch = TOKEN_TILE // CHUNK
    dirs = ((qkv_f, g_f, gt_f, hf_ref, lower, lower_f, upper_f),
            (qkv_b, g_b, gt_b, hb_ref, upper, upper_f, lower_f))
    for ci in range(nch):
        for di, (qkv, g_ref, gt_ref, h_ref, incl, mcum, mcum_t) in enumerate(dirs):
            cc = ci if di == 0 else nch - 1 - ci
            rows = slice(cc * CHUNK, (cc + 1) * CHUNK)
            g = g_ref[rows, 0:N_GATES]
            gt = gt_ref[cc]
            ls = _log_sigmoid(g + fbr)
            lst = _log_sigmoid(gt + fbc)
            bc_all = _dot_exact(mcum, ls)
            br_all = _dot_exact(lst, mcum_t)
            tot = jnp.sum(ls, axis=0, keepdims=True)
            for h in range(M_HEADS):
                ic = di * 8 + h
                fc = di * 8 + 4 + h
                base = h * 2 * M_QK
                q = qkv[rows, base:base + M_QK]
                k = qkv[rows, base + M_QK:base + 2 * M_QK] * (M_QK ** -0.5)
                v = qkv[rows, 2 * M_QKW + h * M_V:2 * M_QKW + (h + 1) * M_V]
                st = di * M_HEADS + h
                hh, c_new, n_new, m_new = _mlstm_chunk(
                    q, k, v, g[:, ic:ic + 1], bc_all[:, fc:fc + 1], gt[ic:ic + 1, :],
                    br_all[fc:fc + 1, :], tot[:, fc:fc + 1],
                    c_scr[st], n_scr[st], m_scr[st], incl)
                c_scr[st] = c_new
                n_scr[st] = n_new
                m_scr[st] = m_new
                h_ref[rows, h * M_V:(h + 1) * M_V] = hh


def _mlstm(pm, gates_t, fb_row, fb_col, *, nct):
    bsz, t, _ = pm.shape
    nt = t // TOKEN_TILE
    nch = TOKEN_TILE // CHUNK
    qkv_w = 2 * M_QKW + M_VW
    fwd = lambda b, s: (b, s, 0)
    bwd = lambda b, s: (b, _bwd_tile(s, nct, nt), 0)
    fwd_g = lambda b, s: (b, s, PM_GATE_BLK)
    bwd_g = lambda b, s: (b, _bwd_tile(s, nct, nt), PM_GATE_BLK)
    fwd_t = lambda b, s: (b, s, 0, 0)
    bwd_t = lambda b, s: (b, _bwd_tile(s, nct, nt), 0, 0)
    out = jax.ShapeDtypeStruct((bsz, t, M_VW), F32)
    return pl.pallas_call(
        _mlstm_kernel,
        grid=(bsz, nt),
        in_specs=[
            pl.BlockSpec((None, TOKEN_TILE, qkv_w), fwd),
            pl.BlockSpec((None, TOKEN_TILE, LANES), fwd_g),
            pl.BlockSpec((None, nch, N_GATES, CHUNK), fwd_t),
            pl.BlockSpec((None, TOKEN_TILE, qkv_w), bwd),
            pl.BlockSpec((None, TOKEN_TILE, LANES), bwd_g),
            pl.BlockSpec((None, nch, N_GATES, CHUNK), bwd_t),
            pl.BlockSpec((1, N_GATES), lambda b, s: (0, 0)),
            pl.BlockSpec((N_GATES, 1), lambda b, s: (0, 0)),
        ],
        out_specs=[pl.BlockSpec((None, TOKEN_TILE, M_VW), fwd),
                   pl.BlockSpec((None, TOKEN_TILE, M_VW), bwd)],
        out_shape=[out, out],
        scratch_shapes=[pltpu.VMEM((2 * M_HEADS, M_QK, M_V), F32),
                        pltpu.VMEM((2 * M_HEADS, 1, M_QK), F32),
                        pltpu.VMEM((2 * M_HEADS, 1, 1), F32)],
        compiler_params=_params(("parallel", "arbitrary")),
        name="mlstm",
    )(pm, pm, gates_t, pm, pm, gates_t, fb_row, fb_col)


def _unit_tri_inverse(a):
    n = a.shape[0]
    row = lax.broadcasted_iota(jnp.int32, (n, n), 0)
    col = lax.broadcasted_iota(jnp.int32, (n, n), 1)
    x = jnp.where(row == col, 1.0, 0.0) - a
    p = a
    for _ in range(int(math.log2(n)) - 1):
        p = _dot_exact(p, p)
        x = x + _dot_exact(x, p)
    return x


def _gdn_chunk(q, k, v, gc_c, gc_r, beta_c, g_end, s_st, incl, strict):
    decay = jnp.where(incl, jnp.exp(jnp.where(incl, gc_c - gc_r, 0.0)), 0.0)
    kb = k * beta_c
    a = jnp.where(strict, _dot_nt(kb, k) * decay, 0.0)
    t_inv = _unit_tri_inverse(a)
    eg = jnp.exp(gc_c)
    u = _dot(t_inv, v * beta_c)
    w = _dot(t_inv, kb * eg)
    k_end = k * jnp.exp(g_end - gc_c)
    qk = jnp.where(incl, _dot_nt(q, k) * decay, 0.0)
    v_new = u - _dot(w, s_st)
    o = _dot(q * eg, s_st) + _dot(qk, v_new)
    s_new = s_st * jnp.exp(g_end) + _dot_tn(k_end, v_new)
    return o, s_new


def _gdn_kernel(qkv_f, g_f, gt_f, qkv_b, g_b, gt_b, par_r, par_c, of_ref, ob_ref, s_scr):
    @pl.when(pl.program_id(1) == 0)
    def _():
        s_scr[...] = jnp.zeros_like(s_scr)

    lower, upper, s_lower, s_upper = _tri_masks(CHUNK)
    lower_f, upper_f = lower.astype(F32), upper.astype(F32)
    alog_r, dtb_r = par_r[0:1, :], par_r[1:2, :]
    alog_c, dtb_c = par_c[:, 0:1], par_c[:, 1:2]
    col_r = lax.broadcasted_iota(jnp.int32, (CHUNK, N_GATES), 1)
    row_c = lax.broadcasted_iota(jnp.int32, (N_GATES, CHUNK), 0)
    nch = TOKEN_TILE // CHUNK
    dirs = ((qkv_f, g_f, gt_f, of_ref, lower, s_lower, lower_f, upper_f),
            (qkv_b, g_b, gt_b, ob_ref, upper, s_upper, upper_f, lower_f))
    for ci in range(nch):
        for di, (qkv, g_ref, gt_ref, o_ref, incl, strict, mcum, mcum_t) in enumerate(dirs):
            cc = ci if di == 0 else nch - 1 - ci
            rows = slice(cc * CHUNK, (cc + 1) * CHUNK)
            g = g_ref[rows, 0:N_GATES]
            gt = gt_ref[cc]
            val = jnp.where(col_r < 8, -jnp.exp(alog_r) * _softplus(g + dtb_r), _sigmoid(g))
            val_t = jnp.where(row_c < 8, -jnp.exp(alog_c) * _softplus(gt + dtb_c), _sigmoid(gt))
            gc_all = _dot_exact(mcum, val)
            gr_all = _dot_exact(val_t, mcum_t)
            tot = jnp.sum(val, axis=0, keepdims=True)
            for h in range(G_HEADS):
                ac = di * 4 + h
                bc = 8 + di * 4 + h
                q = qkv[rows, h * G_HD:(h + 1) * G_HD]
                k = qkv[rows, G_W + h * G_HD:G_W + (h + 1) * G_HD]
                v = qkv[rows, 2 * G_W + h * G_HD:2 * G_W + (h + 1) * G_HD]
                st = di * G_HEADS + h
                o, s_new = _gdn_chunk(q, k, v, gc_all[:, ac:ac + 1], gr_all[ac:ac + 1, :],
                                      val[:, bc:bc + 1], tot[:, ac:ac + 1], s_scr[st], incl, strict)
                s_scr[st] = s_new
                o_ref[rows, h * G_HD:(h + 1) * G_HD] = o


def _gdn(qkv, pg, gates_t, par_row, par_col, *, nct):
    bsz, t, _ = qkv.shape
    nt = t // TOKEN_TILE
    nch = TOKEN_TILE // CHUNK
    fwd = lambda b, s: (b, s, 0)
    bwd = lambda b, s: (b, _bwd_tile(s, nct, nt), 0)
    fwd_g = lambda b, s: (b, s, PG_GATE_BLK)
    bwd_g = lambda b, s: (b, _bwd_tile(s, nct, nt), PG_GATE_BLK)
    fwd_t = lambda b, s: (b, s, 0, 0)
    bwd_t = lambda b, s: (b, _bwd_tile(s, nct, nt), 0, 0)
    out = jax.ShapeDtypeStruct((bsz, t, G_W), F32)
    return pl.pallas_call(
        _gdn_kernel,
        grid=(bsz, nt),
        in_specs=[
            pl.BlockSpec((None, TOKEN_TILE, 3 * G_W), fwd),
            pl.BlockSpec((None, TOKEN_TILE, LANES), fwd_g),
            pl.BlockSpec((None, nch, N_GATES, CHUNK), fwd_t),
            pl.BlockSpec((None, TOKEN_TILE, 3 * G_W), bwd),
            pl.BlockSpec((None, TOKEN_TILE, LANES), bwd_g),
            pl.BlockSpec((None, nch, N_GATES, CHUNK), bwd_t),
            pl.BlockSpec((2, N_GATES), lambda b, s: (0, 0)),
            pl.BlockSpec((N_GATES, 2), lambda b, s: (0, 0)),
        ],
        out_specs=[pl.BlockSpec((None, TOKEN_TILE, G_W), fwd),
                   pl.BlockSpec((None, TOKEN_TILE, G_W), bwd)],
        out_shape=[out, out],
        scratch_shapes=[pltpu.VMEM((2 * G_HEADS, G_HD, G_HD), F32)],
        compiler_params=_params(("parallel", "arbitrary")),
        name="gdn",
    )(qkv, pg, gates_t, qkv, pg, gates_t, par_row, par_col)


def _merge_kernel(x_ref, mod_ref, gain_ref, ya_ref, hf_ref, hb_ref, o_ref, of_ref, ob_ref, z_ref,
                  gs_ref, mn_ref, gn_ref, wb_ref, wo_ref, out_ref):
    hs = hf_ref[...] + hb_ref[...]
    og = o_ref[...]
    os_ = of_ref[...] + ob_ref[...]
    z = z_ref[...]
    ym, yg = [], []
    for h in range(M_HEADS):
        sl = slice(h * M_V, (h + 1) * M_V)
        ym.append(_rms(hs[:, sl], mn_ref[:, sl]) * _sigmoid(og[:, sl]))
    for h in range(G_HEADS):
        sl = slice(h * G_HD, (h + 1) * G_HD)
        zz = z[:, sl]
        yg.append(_rms(os_[:, sl], gn_ref[...]) * (zz * _sigmoid(zz)))
    ym = jnp.concatenate(ym, axis=-1)
    yg = jnp.concatenate(yg, axis=-1)
    gs = gs_ref[...]
    y = (gs[:, 0:D_MODEL] * _dot(ya_ref[...], wb_ref[0])
         + gs[:, D_MODEL:2 * D_MODEL] * _dot(ym, wb_ref[1])
         + gs[:, 2 * D_MODEL:] * _dot(yg, wb_ref[2]))
    y2 = _dot(y, wo_ref[...])
    out_ref[...] = x_ref[...] + mod_ref[2:3, :] * _rms(y2, gain_ref[1:2, :])


def _merge(x_all, mod, gains, ya, hf, hb, pm, of, ob, pg, gsig, m_norm, g_norm, wb, wo, *, nct, t_off):
    bsz, t, _ = x_all.shape
    nt = t // TOKEN_TILE - t_off
    tile = lambda w, blk=0: pl.BlockSpec((None, TOKEN_TILE, w), lambda b, i: (b, i + t_off, blk))
    const = lambda shape: pl.BlockSpec(shape, lambda b, i: (0,) * len(shape))
    return pl.pallas_call(
        _merge_kernel,
        grid=(bsz, nt),
        in_specs=[
            tile(D_MODEL),
            pl.BlockSpec((None, 6, D_MODEL), lambda b, i: (2 * b + (i + t_off >= nct).astype(jnp.int32), 0, 0)),
            const((4, D_MODEL)),
            tile(A_VW), tile(M_VW), tile(M_VW),
            tile(M_VW, (2 * M_QKW + M_VW) // M_VW),
            tile(G_W), tile(G_W),
            tile(G_W, 3),
            tile(3 * D_MODEL),
            const((1, M_VW)), const((1, G_HD)),
            const((3, BRANCH_W, D_MODEL)), const((D_MODEL, D_MODEL)),
        ],
        out_specs=pl.BlockSpec((None, TOKEN_TILE, D_MODEL), lambda b, i: (b, i, 0)),
        out_shape=jax.ShapeDtypeStruct((bsz, nt * TOKEN_TILE, D_MODEL), F32),
        compiler_params=_params(("parallel", "parallel")),
        name="merge",
    )(x_all, mod, gains, ya, hf, hb, pm, of, ob, pg, gsig, m_norm, g_norm, wb, wo)


def _ffn_kernel(x_ref, mod_ref, gain_ref, w1_ref, w2_ref, out_ref):
    x = x_ref[...]
    h = (_rms(x, gain_ref[2:3, :]) * (1.0 + mod_ref[4:5, :]) + mod_ref[3:4, :]).astype(BF16)
    acc = jnp.zeros((x.shape[0], D_MODEL), F32)
    step = 1024
    for j in range(D_FF // step):
        a = jnp.maximum(jnp.dot(h, w1_ref[:, j * step:(j + 1) * step], preferred_element_type=F32), 0.0)
        acc = acc + jnp.dot((a * a).astype(BF16), w2_ref[j * step:(j + 1) * step, :], preferred_element_type=F32)
    out_ref[...] = x + mod_ref[5:6, :] * _rms(acc, gain_ref[3:4, :])


def _ffn(x_in, mod, gains, w1, w2, *, nct, t_off):
    bsz, t, _ = x_in.shape
    nt = t // TOKEN_TILE
    return pl.pallas_call(
        _ffn_kernel,
        grid=(bsz, nt),
        in_specs=[
            pl.BlockSpec((None, TOKEN_TILE, D_MODEL), lambda b, i: (b, i, 0)),
            pl.BlockSpec((None, 6, D_MODEL), lambda b, i: (2 * b + (i + t_off >= nct).astype(jnp.int32), 0, 0)),
            pl.BlockSpec((4, D_MODEL), lambda b, i: (0, 0)),
            pl.BlockSpec((D_MODEL, D_FF), lambda b, i: (0, 0)),
            pl.BlockSpec((D_FF, D_MODEL), lambda b, i: (0, 0)),
        ],
        out_specs=pl.BlockSpec((None, TOKEN_TILE, D_MODEL), lambda b, i: (b, i, 0)),
        out_shape=jax.ShapeDtypeStruct((bsz, t, D_MODEL), F32),
        compiler_params=_params(("parallel", "parallel")),
        name="ffn",
    )(x_in, mod, gains, w1, w2)


def _rope_tables(n_lat):
    rows = n_lat // GRID_W
    row = jnp.repeat(jnp.arange(rows, dtype=F32), GRID_W)
    col = jnp.tile(jnp.arange(GRID_W, dtype=F32), rows)
    half = A_HD // 2
    inv_freq = ROPE_BASE ** (-jnp.arange(0, half, 2, dtype=F32) / half)
    ang_r = row[:, None] * inv_freq
    ang_c = col[:, None] * inv_freq
    ang = jnp.concatenate([ang_r, ang_r, ang_c, ang_c] * 2, axis=-1)
    return jnp.cos(ang), jnp.sin(ang)


def _pad_cols(a, width):
    return jnp.pad(a, [(0, 0)] * (a.ndim - 1) + [(0, width - a.shape[-1])])


def _split_in_proj(w, b):
    def both(f):
        return f(w).astype(BF16), f(b[None, :])

    qk_idx = np.concatenate([np.concatenate([np.arange(_O_MQ + h * M_QK, _O_MQ + (h + 1) * M_QK),
                                             np.arange(_O_MK + h * M_QK, _O_MK + (h + 1) * M_QK)])
                             for h in range(M_HEADS)])
    attn = both(lambda a: a[:, _O_AQ:_O_MQ])
    mlstm = both(lambda a: jnp.concatenate(
        [a[:, qk_idx], a[:, _O_MV:_O_MG], _pad_cols(a[:, _O_MG:_O_GQ], LANES)], axis=-1))
    gdn = both(lambda a: jnp.concatenate(
        [a[:, _O_GQ:_O_GG], _pad_cols(a[:, _O_GG:_O_MRG], LANES)], axis=-1))
    mrg = both(lambda a: a[:, _O_MRG:_O_END])
    return attn, mlstm, gdn, mrg


def kernel(x, c, ctx, c_ctx, w_ada, b_ada, norm_gains, w_in, b_in, a_lambda, a_norm, m_fbias, m_norm,
           g_conv, g_alog, g_dtbias, g_norm, w_branch, w_out, w_ff1, w_ff2):
    bsz, n_lat, _ = x.shape
    tc = ctx.shape[1]
    depth = w_ada.shape[0]
    assert tc % TOKEN_TILE == 0 and n_lat % TOKEN_TILE == 0 and TOKEN_TILE % CHUNK == 0
    assert tc % KEY_TILE == 0 and n_lat % KEY_TILE == 0
    nct = tc // TOKEN_TILE

    rows = -(-(bsz + 1) // 8) * 8
    cond = jnp.zeros((rows, D_MODEL), F32).at[:bsz].set(c).at[bsz].set(c_ctx)
    mod_all = _ada_all(cond, w_ada, b_ada)
    ctx_rows = jnp.broadcast_to(mod_all[:, bsz:bsz + 1], (depth, bsz, 6 * D_MODEL))
    mod_all = jnp.stack([ctx_rows, mod_all[:, :bsz]], axis=2).reshape(depth, 2 * bsz, 6, D_MODEL)

    rope = _rope_tables(n_lat)
    x_all = jnp.concatenate([ctx, x], axis=1)
    zeros4 = jnp.zeros((4,), F32)

    for i in range(depth):
        lam_init = 0.8 - 0.6 * math.exp(-0.3 * i)
        last = i == depth - 1
        mod = mod_all[i]
        gains = norm_gains[i]
        (wa, ba), (wm, bm), (wg, bg), (wr, br) = _split_in_proj(w_in[i], b_in[i])
        g0 = gains[0:1]

        pa = _proj(x_all, mod, g0, wa, ba, epilogue="attn", out_dtype=BF16, nct=nct, rope=rope)
        pm = _proj(x_all, mod, g0, wm, bm, epilogue="plain", out_dtype=F32, nct=nct)
        pg = _proj(x_all, mod, g0, wg, bg, epilogue="plain", out_dtype=F32, nct=nct)
        gsig = _proj(x_all, mod, g0, wr, br, epilogue="sigmoid", out_dtype=F32, nct=nct)

        ya = _attention(pa, a_lambda[i], a_norm[i], nct=nct, tc=tc, lam_init=lam_init)

        fb = jnp.concatenate([zeros4, m_fbias[i, 0], zeros4, m_fbias[i, 1]])
        m_gt = _chunk_gates_t(pm[:, :, PM_GATE_BLK * LANES:PM_GATE_BLK * LANES + N_GATES])
        hf, hb = _mlstm(pm, m_gt, fb[None, :], fb[:, None], nct=nct)

        qkv = _gdn_conv(pg, g_conv[i], tc=tc)
        alog = jnp.concatenate([g_alog[i].reshape(-1), jnp.zeros((8,), F32)])
        dtb = jnp.concatenate([g_dtbias[i].reshape(-1), jnp.zeros((8,), F32)])
        par = jnp.stack([alog, dtb])
        g_gt = _chunk_gates_t(pg[:, :, PG_GATE_BLK * LANES:PG_GATE_BLK * LANES + N_GATES])
        of, ob = _gdn(qkv, pg, g_gt, par, par.T, nct=nct)

        t_off = nct if last else 0
        x_mid = _merge(x_all, mod, gains, ya, hf, hb, pm, of, ob, pg, gsig,
                       m_norm[i].reshape(1, M_VW), g_norm[i].reshape(1, G_HD),
                       w_branch[i].astype(BF16), w_out[i].astype(BF16), nct=nct, t_off=t_off)
        x_all = _ffn(x_mid, mod, gains, w_ff1[i].astype(BF16), w_ff2[i].astype(BF16), nct=nct, t_off=t_off)
    return x_all
```

```python
import functools
import math

import jax
import jax.numpy as jnp
import numpy as np
from jax import lax
from jax.experimental import pallas as pl
from jax.experimental.pallas import tpu as pltpu

F32 = jnp.float32
BF16 = jnp.bfloat16

D_MODEL = 1024
GRID_W = 64
A_HEADS = 4
A_HD = 64
A_VD = 2 * A_HD
A_QK = A_HEADS * 2 * A_HD
A_VW = A_HEADS * A_VD
M_HEADS = 4
M_QK = 64
M_V = 128
M_QKW = M_HEADS * M_QK
M_VW = M_HEADS * M_V
G_HEADS = 4
G_HD = 128
G_W = G_HEADS * G_HD
CONV_K = 5
BRANCH_W = 512
D_FF = 4 * D_MODEL
ROPE_BASE = 10000.0
EPS = 1e-6
N_GATES = 16

LANES = 128
TOKEN_TILE = 256
CHUNK = 64
VMEM_LIMIT = 56 * 1024 * 1024

_OFF = np.cumsum([0, A_QK, A_QK, A_VW, M_QKW, M_QKW, M_VW, M_VW, 4 * M_HEADS,
                  G_W, G_W, G_W, G_W, 4 * G_HEADS, 3 * D_MODEL])
(_O_AQ, _O_AK, _O_AV, _O_MQ, _O_MK, _O_MV, _O_MO, _O_MG,
 _O_GQ, _O_GK, _O_GV, _O_GZ, _O_GG, _O_MRG, _O_END) = [int(v) for v in _OFF]

PA_W = 3 * A_QK
PM_W = M_QKW * 2 + 2 * M_VW + LANES
PM_GATE_BLK = (M_QKW * 2 + 2 * M_VW) // LANES
PG_W = 4 * G_W + LANES
PG_GATE_BLK = 4 * G_W // LANES


def _params(sem):
    return pltpu.CompilerParams(dimension_semantics=sem, vmem_limit_bytes=VMEM_LIMIT)


def _sigmoid(x):
    return 1.0 / (1.0 + jnp.exp(-x))


def _softplus(x):
    return jnp.maximum(x, 0.0) + jnp.log1p(jnp.exp(-jnp.abs(x)))


def _log_sigmoid(x):
    return -_softplus(-x)


def _rms(x, gain):
    return x * lax.rsqrt(jnp.mean(x * x, axis=-1, keepdims=True) + EPS) * gain


def _dot(a, b):
    return jnp.dot(a.astype(BF16), b.astype(BF16), preferred_element_type=F32)


def _dot_nt(a, b):
    return lax.dot_general(a.astype(BF16), b.astype(BF16), (((1,), (1,)), ((), ())),
                           preferred_element_type=F32)


def _dot_tn(a, b):
    return lax.dot_general(a.astype(BF16), b.astype(BF16), (((0,), (0,)), ((), ())),
                           preferred_element_type=F32)


def _dot_exact(a, b):
    return jnp.dot(a, b, precision=lax.Precision.HIGHEST, preferred_element_type=F32)


def _ada_kernel(c_ref, w_ref, b_ref, o_ref):
    c = c_ref[...]
    s = c * _sigmoid(c)
    o_ref[...] = _dot_exact(s, w_ref[...]) + b_ref[...]


def _ada_all(cond, w_ada, b_ada):
    depth = w_ada.shape[0]
    rows = cond.shape[0]
    n = w_ada.shape[2]
    tn = 1024
    return pl.pallas_call(
        _ada_kernel,
        grid=(depth, n // tn),
        in_specs=[pl.BlockSpec((rows, D_MODEL), lambda l, j: (0, 0)),
                  pl.BlockSpec((None, D_MODEL, tn), lambda l, j: (l, 0, j)),
                  pl.BlockSpec((None, 1, tn), lambda l, j: (l, 0, j))],
        out_specs=pl.BlockSpec((None, rows, tn), lambda l, j: (l, 0, j)),
        out_shape=jax.ShapeDtypeStruct((depth, rows, n), F32),
        compiler_params=_params(("parallel", "parallel")),
        name="ada",
    )(cond, w_ada, b_ada.reshape(depth, 1, n))


def _rot_half(x):
    lane = lax.broadcasted_iota(jnp.int32, x.shape, 1)
    first = (lane % 32) < 16
    return jnp.where(first, -pltpu.roll(x, LANES - 16, 1), pltpu.roll(x, 16, 1))


def _proj_kernel(x_ref, mod_ref, g_ref, w_ref, b_ref, *rest, epilogue, nct, t_off):
    x = x_ref[...]
    h = _rms(x, g_ref[...]) * (1.0 + mod_ref[1:2, :]) + mod_ref[0:1, :]
    acc = jnp.dot(h.astype(BF16), w_ref[...], preferred_element_type=F32) + b_ref[...]
    if epilogue == "attn":
        cos_ref, sin_ref, o_ref = rest
        is_lat = (pl.program_id(1) + t_off) >= nct
        cos = jnp.where(is_lat, cos_ref[...], 1.0)
        sin = jnp.where(is_lat, sin_ref[...], 0.0)
        for j in range(2 * A_QK // LANES):
            blk = acc[:, j * LANES:(j + 1) * LANES]
            r = blk * cos + _rot_half(blk) * sin
            if j < A_QK // LANES:
                r = r * (A_HD ** -0.5)
            o_ref[:, j * LANES:(j + 1) * LANES] = r.astype(o_ref.dtype)
        o_ref[:, 2 * A_QK:] = acc[:, 2 * A_QK:].astype(o_ref.dtype)
    elif epilogue == "sigmoid":
        (o_ref,) = rest
        o_ref[...] = _sigmoid(acc).astype(o_ref.dtype)
    else:
        (o_ref,) = rest
        o_ref[...] = acc.astype(o_ref.dtype)


def _proj(x_all, mod, gain, w, b, *, epilogue, out_dtype, nct, rope=None):
    bsz, t, _ = x_all.shape
    n = w.shape[1]
    nt = t // TOKEN_TILE
    in_specs = [
        pl.BlockSpec((None, TOKEN_TILE, D_MODEL), lambda i, j: (i, j, 0)),
        pl.BlockSpec((None, 6, D_MODEL), lambda i, j: (2 * i + (j >= nct).astype(jnp.int32), 0, 0)),
        pl.BlockSpec((1, D_MODEL), lambda i, j: (0, 0)),
        pl.BlockSpec((D_MODEL, n), lambda i, j: (0, 0)),
        pl.BlockSpec((1, n), lambda i, j: (0, 0)),
    ]
    args = [x_all, mod, gain, w, b]
    if rope is not None:
        spec = pl.BlockSpec((TOKEN_TILE, LANES), lambda i, j: (jnp.maximum(j - nct, 0), 0))
        in_specs += [spec, spec]
        args += list(rope)
    return pl.pallas_call(
        functools.partial(_proj_kernel, epilogue=epilogue, nct=nct, t_off=0),
        grid=(bsz, nt),
        in_specs=in_specs,
        out_specs=pl.BlockSpec((None, TOKEN_TILE, n), lambda i, j: (i, j, 0)),
        out_shape=jax.ShapeDtypeStruct((bsz, t, n), out_dtype),
        compiler_params=_params(("parallel", "parallel")),
        name="proj_" + epilogue,
    )(*args)


def _attn_kernel(q_ref, k_ref, v_ref, lam_ref, an_ref, o_ref, *, nct, tc, t_all, lam_init):
    qi = pl.program_id(2)
    q = q_ref[...]
    lane = lax.broadcasted_iota(jnp.int32, q.shape, 1)
    zero = jnp.zeros_like(q)
    qm = (jnp.where(lane < A_HD, q, zero), jnp.where(lane >= A_HD, q, zero))
    lv = lam_ref[...].astype(F32)
    lam = (jnp.exp(jnp.sum(lv[0:1] * lv[1:2], axis=-1, keepdims=True))
           - jnp.exp(jnp.sum(lv[2:3] * lv[3:4], axis=-1, keepdims=True)) + lam_init)

    def attend(n_keys):
        pv = []
        for mp in range(2):
            s = lax.dot_general(qm[mp], k_ref[0:n_keys, :], (((1,), (1,)), ((), ())),
                                preferred_element_type=F32)
            p = jnp.exp(s - jnp.max(s, axis=-1, keepdims=True))
            l = jnp.sum(p, axis=-1, keepdims=True)
            pv.append(jnp.dot(p.astype(BF16), v_ref[0:n_keys, :], preferred_element_type=F32) / l)
        o = pv[0] - lam * pv[1]
        o_ref[...] = _rms(o, an_ref[...]) * (1.0 - lam_init)

    @pl.when(qi < nct)
    def _():
        attend(tc)

    @pl.when(qi >= nct)
    def _():
        attend(t_all)


def _attention(pa, a_lambda, a_norm, *, nct, tc, lam_init):
    bsz, t, _ = pa.shape
    nt = t // TOKEN_TILE
    kb = A_QK // LANES
    return pl.pallas_call(
        functools.partial(_attn_kernel, nct=nct, tc=tc, t_all=t, lam_init=lam_init),
        grid=(bsz, A_HEADS, nt),
        in_specs=[
            pl.BlockSpec((None, TOKEN_TILE, LANES), lambda b, h, i: (b, i, h)),
            pl.BlockSpec((None, t, LANES), lambda b, h, i: (b, 0, kb + h)),
            pl.BlockSpec((None, t, LANES), lambda b, h, i: (b, 0, 2 * kb + h)),
            pl.BlockSpec((4, A_HD), lambda b, h, i: (0, 0)),
            pl.BlockSpec((1, A_VD), lambda b, h, i: (0, 0)),
        ],
        out_specs=pl.BlockSpec((None, TOKEN_TILE, LANES), lambda b, h, i: (b, i, h)),
        out_shape=jax.ShapeDtypeStruct((bsz, t, A_VW), F32),
        compiler_params=_params(("parallel", "parallel", "arbitrary")),
        name="diff_attn",
    )(pa, pa, pa, a_lambda, a_norm.reshape(1, A_VD))


def _conv_kernel(x_ref, w_ref, o_ref, *, tc):
    j = pl.program_id(1)
    x = x_ref[...]
    t_all = x.shape[0]
    w = w_ref[...]
    row = lax.broadcasted_iota(jnp.int32, x.shape, 0)
    in_ctx = row < tc
    pos = jnp.where(in_ctx, row, row - tc)
    seg = jnp.where(in_ctx, tc, t_all - tc)
    half = CONV_K // 2
    acc = x * w[half:half + 1, :]
    for d in range(-half, half + 1):
        if d == 0:
            continue
        shifted = pltpu.roll(x, (-d) % t_all, 0)
        ok = (pos + d >= 0) & (pos + d < seg)
        acc = acc + jnp.where(ok, shifted, 0.0) * w[half + d:half + d + 1, :]
    y = acc * _sigmoid(acc)
    nrm = y * lax.rsqrt(jnp.sum(y * y, axis=-1, keepdims=True) + EPS)
    nq = G_W // LANES
    o_ref[...] = jnp.where(j < nq, nrm * (G_HD ** -0.5), jnp.where(j < 2 * nq, nrm, y))


def _gdn_conv(pg, conv_w, *, tc):
    bsz, t, _ = pg.shape
    nblk = 3 * G_W // LANES
    return pl.pallas_call(
        functools.partial(_conv_kernel, tc=tc),
        grid=(bsz, nblk),
        in_specs=[pl.BlockSpec((None, t, LANES), lambda b, j: (b, 0, j)),
                  pl.BlockSpec((CONV_K, LANES), lambda b, j: (0, j))],
        out_specs=pl.BlockSpec((None, t, LANES), lambda b, j: (b, 0, j)),
        out_shape=jax.ShapeDtypeStruct((bsz, t, 3 * G_W), F32),
        compiler_params=_params(("parallel", "parallel")),
        name="gdn_conv",
    )(pg, conv_w)


def _bwd_tile(s, nct, nt):
    return jnp.where(s < nct, nct - 1 - s, nt - 1 - (s - nct))


def _split2(x):
    hi = x.astype(BF16)
    return hi, (x - hi.astype(F32)).astype(BF16)


def _split3(x):
    hi = x.astype(BF16)
    r = x - hi.astype(F32)
    mid = r.astype(BF16)
    return hi, mid, (r - mid.astype(F32)).astype(BF16)


def _dot_f32ish(a2, b2):
    (ah, al), (bh, bl) = a2, b2
    return (jnp.dot(ah, bl, preferred_element_type=F32) + jnp.dot(al, bh, preferred_element_type=F32)
            + jnp.dot(ah, bh, preferred_element_type=F32))


def _sel_dot(sel, x):
    hi, mid, lo = _split3(x)
    return (jnp.dot(sel, lo, preferred_element_type=F32) + jnp.dot(sel, mid, preferred_element_type=F32)
            + jnp.dot(sel, hi, preferred_element_type=F32))


def _dot_sel(x, sel):
    hi, mid, lo = _split3(x)
    return (jnp.dot(lo, sel, preferred_element_type=F32) + jnp.dot(mid, sel, preferred_element_type=F32)
            + jnp.dot(hi, sel, preferred_element_type=F32))


def _recurrent_call(kernel_fn, name, qkv, qkv_w, gate_src, gate_blk, gates_t, small, out_w, scratch, *, nct):
    bsz, t, _ = qkv.shape
    nt = t // TOKEN_TILE
    fwd = lambda b, s: (b, s, 0)
    bwd = lambda b, s: (b, _bwd_tile(s, nct, nt), 0)
    fwd_g = lambda b, s: (b, s, gate_blk)
    bwd_g = lambda b, s: (b, _bwd_tile(s, nct, nt), gate_blk)
    fwd_t = lambda b, s: (b, 0, s)
    bwd_t = lambda b, s: (b, 0, _bwd_tile(s, nct, nt))
    out = jax.ShapeDtypeStruct((bsz, t, out_w), F32)
    return pl.pallas_call(
        kernel_fn,
        grid=(bsz, nt),
        in_specs=[
            pl.BlockSpec((None, TOKEN_TILE, qkv_w), fwd),
            pl.BlockSpec((None, TOKEN_TILE, LANES), fwd_g),
            pl.BlockSpec((None, N_GATES, TOKEN_TILE), fwd_t),
            pl.BlockSpec((None, TOKEN_TILE, qkv_w), bwd),
            pl.BlockSpec((None, TOKEN_TILE, LANES), bwd_g),
            pl.BlockSpec((None, N_GATES, TOKEN_TILE), bwd_t),
        ] + [pl.BlockSpec(a.shape, lambda b, s: (0, 0)) for a in small],
        out_specs=[pl.BlockSpec((None, TOKEN_TILE, out_w), fwd),
                   pl.BlockSpec((None, TOKEN_TILE, out_w), bwd)],
        out_shape=[out, out],
        scratch_shapes=scratch,
        compiler_params=_params(("parallel", "arbitrary")),
        name=name,
    )(qkv, gate_src, gates_t, qkv, gate_src, gates_t, *small)


def _mlstm_kernel(qkv_f, g_f, gt_f, qkv_b, g_b, gt_b, fbr_ref, fbc_ref, hf_ref, hb_ref,
                  c_scr, n_scr, m_scr):
    @pl.when(pl.program_id(1) == 0)
    def _():
        c_scr[...] = jnp.zeros_like(c_scr)
        n_scr[...] = jnp.zeros_like(n_scr)
        m_scr[...] = jnp.zeros_like(m_scr)

    n = TOKEN_TILE
    row = lax.broadcasted_iota(jnp.int32, (n, n), 0)
    col = lax.broadcasted_iota(jnp.int32, (n, n), 1)
    lower, upper = col <= row, col >= row
    lower_s, upper_s = lower.astype(BF16), upper.astype(BF16)
    fbr = fbr_ref[...]
    fbc = fbc_ref[...]
    dirs = ((qkv_f, g_f, gt_f, hf_ref, lower, lower_s, upper_s),
            (qkv_b, g_b, gt_b, hb_ref, upper, upper_s, lower_s))

    probs = []
    for di, (qkv, g_ref, gt_ref, h_ref, incl, csum, csum_t) in enumerate(dirs):
        g = g_ref[:, 0:N_GATES]
        gt = gt_ref[...]
        ls = _log_sigmoid(g + fbr)
        lst = _log_sigmoid(gt + fbc)
        bc_all = _sel_dot(csum, ls)
        br_all = _dot_sel(lst, csum_t)
        tot = jnp.sum(ls, axis=0, keepdims=True)
        for h in range(M_HEADS):
            ic = di * 8 + h
            fc = di * 8 + 4 + h
            base = h * 2 * M_QK
            st = di * M_HEADS + h
            probs.append(dict(
                st=st, h=h, h_ref=h_ref, incl=incl,
                q=qkv[:, base:base + M_QK],
                k=qkv[:, base + M_QK:base + 2 * M_QK] * (M_QK ** -0.5),
                v=qkv[:, 2 * M_QKW + h * M_V:2 * M_QKW + (h + 1) * M_V],
                i_c=g[:, ic:ic + 1], b_c=bc_all[:, fc:fc + 1], i_r=gt[ic:ic + 1, :],
                b_r=br_all[fc:fc + 1, :], btot=tot[:, fc:fc + 1],
                c=c_scr[st], n=n_scr[st], m=m_scr[st]))
    for p in probs:
        p["qk"] = _dot_nt(p["q"], p["k"])
        p["qc"] = _dot(p["q"], p["c"])
    for p in probs:
        d = jnp.where(p["incl"], p["b_c"] - p["b_r"] + p["i_r"], -jnp.inf)
        inter = p["b_c"] + p["m"]
        p["m_row"] = jnp.maximum(inter, jnp.max(d, axis=-1, keepdims=True))
        p["e_inter"] = jnp.exp(inter - p["m_row"])
        p["s"] = p["qk"] * jnp.exp(d - p["m_row"])
    for p in probs:
        a_r = p["btot"] - p["b_r"] + p["i_r"]
        a_c = p["btot"] - p["b_c"] + p["i_c"]
        m_new = jnp.maximum(p["btot"] + p["m"], jnp.max(a_r, axis=-1, keepdims=True))
        p["carry_w"] = jnp.exp(p["btot"] + p["m"] - m_new)
        p["kw"] = p["k"] * jnp.exp(a_c - m_new)
        m_scr[p["st"]] = m_new
    for p in probs:
        num = _dot(p["s"], p["v"]) + p["e_inter"] * p["qc"]
        den = (jnp.sum(p["s"], axis=-1, keepdims=True)
               + p["e_inter"] * jnp.sum(p["q"] * p["n"], axis=-1, keepdims=True))
        hh = num / jnp.maximum(jnp.abs(den), jnp.exp(-p["m_row"]))
        p["h_ref"][:, p["h"] * M_V:(p["h"] + 1) * M_V] = hh
    for p in probs:
        c_scr[p["st"]] = p["carry_w"] * p["c"] + _dot_tn(p["kw"], p["v"])
        n_scr[p["st"]] = p["carry_w"] * p["n"] + jnp.sum(p["kw"], axis=0, keepdims=True)


def _mlstm(pm, gates_t, fb_row, fb_col, *, nct):
    scratch = [pltpu.VMEM((2 * M_HEADS, M_QK, M_V), F32),
               pltpu.VMEM((2 * M_HEADS, 1, M_QK), F32),
               pltpu.VMEM((2 * M_HEADS, 1, 1), F32)]
    return _recurrent_call(_mlstm_kernel, "mlstm", pm, 2 * M_QKW + M_VW, pm, PM_GATE_BLK, gates_t,
                           [fb_row, fb_col], M_VW, scratch, nct=nct)


def _unit_tri_inverse_many(a_list, eye):
    xs = [eye - a for a in a_list]
    ps = [_split2(a) for a in a_list]
    for _ in range(int(math.log2(CHUNK)) - 1):
        ps = [_split2(_dot_f32ish(p2, p2)) for p2 in ps]
        xs = [x + _dot_f32ish(_split2(x), p2) for x, p2 in zip(xs, ps)]
    return xs


def _gdn_kernel(qkv_f, g_f, gt_f, qkv_b, g_b, gt_b, par_r, par_c, of_ref, ob_ref, s_scr):
    @pl.when(pl.program_id(1) == 0)
    def _():
        s_scr[...] = jnp.zeros_like(s_scr)

    n = TOKEN_TILE
    nch = n // CHUNK
    row = lax.broadcasted_iota(jnp.int32, (n, n), 0)
    col = lax.broadcasted_iota(jnp.int32, (n, n), 1)
    same = (row // CHUNK) == (col // CHUNK)
    lower, upper = same & (col <= row), same & (col >= row)
    lower_x, upper_x = same & (col < row), same & (col > row)
    lower_s, upper_s, same_s = lower.astype(BF16), upper.astype(BF16), same.astype(BF16)
    eye = jnp.where(row == col, 1.0, 0.0)
    alog_r, dtb_r = par_r[0:1, :], par_r[1:2, :]
    alog_c, dtb_c = par_c[:, 0:1], par_c[:, 1:2]
    col_r = lax.broadcasted_iota(jnp.int32, (n, N_GATES), 1)
    row_c = lax.broadcasted_iota(jnp.int32, (N_GATES, n), 0)
    dirs = ((qkv_f, g_f, gt_f, of_ref, lower, lower_x, lower_s, upper_s),
            (qkv_b, g_b, gt_b, ob_ref, upper, upper_x, upper_s, lower_s))

    probs = []
    for di, (qkv, g_ref, gt_ref, o_ref, incl, strict, csum, csum_t) in enumerate(dirs):
        g = g_ref[:, 0:N_GATES]
        gt = gt_ref[...]
        val = jnp.where(col_r < 8, -jnp.exp(alog_r) * _softplus(g + dtb_r), _sigmoid(g))
        val_t = jnp.where(row_c < 8, -jnp.exp(alog_c) * _softplus(gt + dtb_c), _sigmoid(gt))
        gc_all = _sel_dot(csum, val)
        gr_all = _dot_sel(val_t, csum_t)
        ge_all = _sel_dot(same_s, val)
        for h in range(G_HEADS):
            ac = di * 4 + h
            bc = 8 + di * 4 + h
            q = qkv[:, h * G_HD:(h + 1) * G_HD]
            k = qkv[:, G_W + h * G_HD:G_W + (h + 1) * G_HD]
            v = qkv[:, 2 * G_W + h * G_HD:2 * G_W + (h + 1) * G_HD]
            gc_c, gc_r = gc_all[:, ac:ac + 1], gr_all[ac:ac + 1, :]
            beta_c, ge_c = val[:, bc:bc + 1], ge_all[:, ac:ac + 1]
            decay = jnp.where(incl, jnp.exp(jnp.where(incl, gc_c - gc_r, 0.0)), 0.0)
            kb = k * beta_c
            eg = jnp.exp(gc_c)
            probs.append(dict(
                di=di, h=h, o_ref=o_ref,
                a=jnp.where(strict, _dot_nt(kb, k) * decay, 0.0),
                rhs=jnp.concatenate([v * beta_c, kb * eg], axis=-1),
                k_end=k * jnp.exp(ge_c - gc_c),
                qk=jnp.where(incl, _dot_nt(q, k) * decay, 0.0),
                qg=q * eg,
                carry=jnp.exp(ge_c),
                s=s_scr[di * G_HEADS + h]))
    t_inv = _unit_tri_inverse_many([p["a"] for p in probs], eye)
    for p, t in zip(probs, t_inv):
        uw = _dot(t, p["rhs"])
        p["u"], p["w"] = uw[:, 0:G_HD], uw[:, G_HD:]
        p["v_new"] = [None] * nch
        p["o_inter"] = [None] * nch

    for ci in range(nch):
        for p in probs:
            cc = ci if p["di"] == 0 else nch - 1 - ci
            rows = slice(cc * CHUNK, (cc + 1) * CHUNK)
            s_st = p["s"]
            v_new = p["u"][rows] - _dot(p["w"][rows], s_st)
            p["o_inter"][cc] = _dot(p["qg"][rows], s_st)
            p["v_new"][cc] = v_new
            p["s"] = s_st * p["carry"][cc * CHUNK:cc * CHUNK + 1, :] + _dot_tn(p["k_end"][rows], v_new)
    for p in probs:
        o = jnp.concatenate(p["o_inter"], axis=0) + _dot(p["qk"], jnp.concatenate(p["v_new"], axis=0))
        s_scr[p["di"] * G_HEADS + p["h"]] = p["s"]
        p["o_ref"][:, p["h"] * G_HD:(p["h"] + 1) * G_HD] = o


def _gdn(qkv, pg, gates_t, par_row, par_col, *, nct):
    scratch = [pltpu.VMEM((2 * G_HEADS, G_HD, G_HD), F32)]
    return _recurrent_call(_gdn_kernel, "gdn", qkv, 3 * G_W, pg, PG_GATE_BLK, gates_t,
                           [par_row, par_col], G_W, scratch, nct=nct)


def _merge_kernel(x_ref, mod_ref, gain_ref, ya_ref, hf_ref, hb_ref, o_ref, of_ref, ob_ref, z_ref,
                  gs_ref, mn_ref, gn_ref, wb_ref, wo_ref, out_ref):
    hs = hf_ref[...] + hb_ref[...]
    og = o_ref[...]
    os_ = of_ref[...] + ob_ref[...]
    z = z_ref[...]
    ym, yg = [], []
    for h in range(M_HEADS):
        sl = slice(h * M_V, (h + 1) * M_V)
        ym.append(_rms(hs[:, sl], mn_ref[:, sl]) * _sigmoid(og[:, sl]))
    for h in range(G_HEADS):
        sl = slice(h * G_HD, (h + 1) * G_HD)
        zz = z[:, sl]
        yg.append(_rms(os_[:, sl], gn_ref[...]) * (zz * _sigmoid(zz)))
    ym = jnp.concatenate(ym, axis=-1)
    yg = jnp.concatenate(yg, axis=-1)
    gs = gs_ref[...]
    y = (gs[:, 0:D_MODEL] * _dot(ya_ref[...], wb_ref[0])
         + gs[:, D_MODEL:2 * D_MODEL] * _dot(ym, wb_ref[1])
         + gs[:, 2 * D_MODEL:] * _dot(yg, wb_ref[2]))
    y2 = _dot(y, wo_ref[...])
    out_ref[...] = x_ref[...] + mod_ref[2:3, :] * _rms(y2, gain_ref[1:2, :])


def _merge(x_all, mod, gains, ya, hf, hb, pm, of, ob, pg, gsig, m_norm, g_norm, wb, wo, *, nct, t_off):
    bsz, t, _ = x_all.shape
    nt = t // TOKEN_TILE - t_off
    tile = lambda w, blk=0: pl.BlockSpec((None, TOKEN_TILE, w), lambda b, i: (b, i + t_off, blk))
    const = lambda shape: pl.BlockSpec(shape, lambda b, i: (0,) * len(shape))
    return pl.pallas_call(
        _merge_kernel,
        grid=(bsz, nt),
        in_specs=[
            tile(D_MODEL),
            pl.BlockSpec((None, 6, D_MODEL), lambda b, i: (2 * b + (i + t_off >= nct).astype(jnp.int32), 0, 0)),
            const((4, D_MODEL)),
            tile(A_VW), tile(M_VW), tile(M_VW),
            tile(M_VW, (2 * M_QKW + M_VW) // M_VW),
            tile(G_W), tile(G_W),
            tile(G_W, 3),
            tile(3 * D_MODEL),
            const((1, M_VW)), const((1, G_HD)),
            const((3, BRANCH_W, D_MODEL)), const((D_MODEL, D_MODEL)),
        ],
        out_specs=pl.BlockSpec((None, TOKEN_TILE, D_MODEL), lambda b, i: (b, i, 0)),
        out_shape=jax.ShapeDtypeStruct((bsz, nt * TOKEN_TILE, D_MODEL), F32),
        compiler_params=_params(("parallel", "parallel")),
        name="merge",
    )(x_all, mod, gains, ya, hf, hb, pm, of, ob, pg, gsig, m_norm, g_norm, wb, wo)


def _ffn_kernel(x_ref, mod_ref, gain_ref, w1_ref, w2_ref, out_ref):
    x = x_ref[...]
    h = (_rms(x, gain_ref[2:3, :]) * (1.0 + mod_ref[4:5, :]) + mod_ref[3:4, :]).astype(BF16)
    acc = jnp.zeros((x.shape[0], D_MODEL), F32)
    step = 1024
    for j in range(D_FF // step):
        a = jnp.maximum(jnp.dot(h, w1_ref[:, j * step:(j + 1) * step], preferred_element_type=F32), 0.0)
        acc = acc + jnp.dot((a * a).astype(BF16), w2_ref[j * step:(j + 1) * step, :], preferred_element_type=F32)
    out_ref[...] = x + mod_ref[5:6, :] * _rms(acc, gain_ref[3:4, :])


def _ffn(x_in, mod, gains, w1, w2, *, nct, t_off):
    bsz, t, _ = x_in.shape
    nt = t // TOKEN_TILE
    return pl.pallas_call(
        _ffn_kernel,
        grid=(bsz, nt),
        in_specs=[
            pl.BlockSpec((None, TOKEN_TILE, D_MODEL), lambda b, i: (b, i, 0)),
            pl.BlockSpec((None, 6, D_MODEL), lambda b, i: (2 * b + (i + t_off >= nct).astype(jnp.int32), 0, 0)),
            pl.BlockSpec((4, D_MODEL), lambda b, i: (0, 0)),
            pl.BlockSpec((D_MODEL, D_FF), lambda b, i: (0, 0)),
            pl.BlockSpec((D_FF, D_MODEL), lambda b, i: (0, 0)),
        ],
        out_specs=pl.BlockSpec((None, TOKEN_TILE, D_MODEL), lambda b, i: (b, i, 0)),
        out_shape=jax.ShapeDtypeStruct((bsz, t, D_MODEL), F32),
        compiler_params=_params(("parallel", "parallel")),
        name="ffn",
    )(x_in, mod, gains, w1, w2)


def _rope_tables(n_lat):
    rows = n_lat // GRID_W
    row = jnp.repeat(jnp.arange(rows, dtype=F32), GRID_W)
    col = jnp.tile(jnp.arange(GRID_W, dtype=F32), rows)
    half = A_HD // 2
    inv_freq = ROPE_BASE ** (-jnp.arange(0, half, 2, dtype=F32) / half)
    ang_r = row[:, None] * inv_freq
    ang_c = col[:, None] * inv_freq
    ang = jnp.concatenate([ang_r, ang_r, ang_c, ang_c] * 2, axis=-1)
    return jnp.cos(ang), jnp.sin(ang)


def _pad_cols(a, width):
    return jnp.pad(a, [(0, 0)] * (a.ndim - 1) + [(0, width - a.shape[-1])])


def _split_in_proj(w, b):
    def both(f):
        return f(w).astype(BF16), f(b[None, :])

    qk_idx = np.concatenate([np.concatenate([np.arange(_O_MQ + h * M_QK, _O_MQ + (h + 1) * M_QK),
                                             np.arange(_O_MK + h * M_QK, _O_MK + (h + 1) * M_QK)])
                             for h in range(M_HEADS)])
    attn = both(lambda a: a[:, _O_AQ:_O_MQ])
    mlstm = both(lambda a: jnp.concatenate(
        [a[:, qk_idx], a[:, _O_MV:_O_MG], _pad_cols(a[:, _O_MG:_O_GQ], LANES)], axis=-1))
    gdn = both(lambda a: jnp.concatenate(
        [a[:, _O_GQ:_O_GG], _pad_cols(a[:, _O_GG:_O_MRG], LANES)], axis=-1))
    mrg = both(lambda a: a[:, _O_MRG:_O_END])
    return attn, mlstm, gdn, mrg


def kernel(x, c, ctx, c_ctx, w_ada, b_ada, norm_gains, w_in, b_in, a_lambda, a_norm, m_fbias, m_norm,
           g_conv, g_alog, g_dtbias, g_norm, w_branch, w_out, w_ff1, w_ff2):
    bsz, n_lat, _ = x.shape
    tc = ctx.shape[1]
    depth = w_ada.shape[0]
    assert tc % TOKEN_TILE == 0 and n_lat % TOKEN_TILE == 0 and TOKEN_TILE % CHUNK == 0
    nct = tc // TOKEN_TILE

    rows = -(-(bsz + 1) // 8) * 8
    cond = jnp.zeros((rows, D_MODEL), F32).at[:bsz].set(c).at[bsz].set(c_ctx)
    mod_all = _ada_all(cond, w_ada, b_ada)
    ctx_rows = jnp.broadcast_to(mod_all[:, bsz:bsz + 1], (depth, bsz, 6 * D_MODEL))
    mod_all = jnp.stack([ctx_rows, mod_all[:, :bsz]], axis=2).reshape(depth, 2 * bsz, 6, D_MODEL)

    rope = _rope_tables(n_lat)
    x_all = jnp.concatenate([ctx, x], axis=1)
    zeros4 = jnp.zeros((4,), F32)

    for i in range(depth):
        lam_init = 0.8 - 0.6 * math.exp(-0.3 * i)
        last = i == depth - 1
        mod = mod_all[i]
        gains = norm_gains[i]
        (wa, ba), (wm, bm), (wg, bg), (wr, br) = _split_in_proj(w_in[i], b_in[i])
        g0 = gains[0:1]

        pa = _proj(x_all, mod, g0, wa, ba, epilogue="attn", out_dtype=BF16, nct=nct, rope=rope)
        pm = _proj(x_all, mod, g0, wm, bm, epilogue="plain", out_dtype=F32, nct=nct)
        pg = _proj(x_all, mod, g0, wg, bg, epilogue="plain", out_dtype=F32, nct=nct)
        gsig = _proj(x_all, mod, g0, wr, br, epilogue="sigmoid", out_dtype=F32, nct=nct)

        ya = _attention(pa, a_lambda[i], a_norm[i], nct=nct, tc=tc, lam_init=lam_init)

        fb = jnp.concatenate([zeros4, m_fbias[i, 0], zeros4, m_fbias[i, 1]])
        m_gt = jnp.swapaxes(pm[:, :, PM_GATE_BLK * LANES:PM_GATE_BLK * LANES + N_GATES], 1, 2)
        hf, hb = _mlstm(pm, m_gt, fb[None, :], fb[:, None], nct=nct)

        qkv = _gdn_conv(pg, g_conv[i], tc=tc)
        alog = jnp.concatenate([g_alog[i].reshape(-1), jnp.zeros((8,), F32)])
        dtb = jnp.concatenate([g_dtbias[i].reshape(-1), jnp.zeros((8,), F32)])
        par = jnp.stack([alog, dtb])
        g_gt = jnp.swapaxes(pg[:, :, PG_GATE_BLK * LANES:PG_GATE_BLK * LANES + N_GATES], 1, 2)
        of, ob = _gdn(qkv, pg, g_gt, par, par.T, nct=nct)

        t_off = nct if last else 0
        x_mid = _merge(x_all, mod, gains, ya, hf, hb, pm, of, ob, pg, gsig,
                       m_norm[i].reshape(1, M_VW), g_norm[i].reshape(1, G_HD),
                       w_branch[i].astype(BF16), w_out[i].astype(BF16), nct=nct, t_off=t_off)
        x_all = _ffn(x_mid, mod, gains, w_ff1[i].astype(BF16), w_ff2[i].astype(BF16), nct=nct, t_off=t_off)
    return x_all
```

```python
import functools
import math

import jax
import jax.numpy as jnp
import numpy as np
from jax import lax
from jax.experimental import pallas as pl
from jax.experimental.pallas import tpu as pltpu

F32 = jnp.float32
BF16 = jnp.bfloat16

D_MODEL = 1024
GRID_W = 64
A_HEADS = 4
A_HD = 64
A_VD = 2 * A_HD
A_QK = A_HEADS * 2 * A_HD
A_VW = A_HEADS * A_VD
M_HEADS = 4
M_QK = 64
M_V = 128
M_QKW = M_HEADS * M_QK
M_VW = M_HEADS * M_V
G_HEADS = 4
G_HD = 128
G_W = G_HEADS * G_HD
CONV_K = 5
BRANCH_W = 512
D_FF = 4 * D_MODEL
ROPE_BASE = 10000.0
EPS = 1e-6
N_GATES = 16

LANES = 128
TOKEN_TILE = 256
CHUNK = 64
VMEM_LIMIT = 56 * 1024 * 1024

_OFF = np.cumsum([0, A_QK, A_QK, A_VW, M_QKW, M_QKW, M_VW, M_VW, 4 * M_HEADS,
                  G_W, G_W, G_W, G_W, 4 * G_HEADS, 3 * D_MODEL])
(_O_AQ, _O_AK, _O_AV, _O_MQ, _O_MK, _O_MV, _O_MO, _O_MG,
 _O_GQ, _O_GK, _O_GV, _O_GZ, _O_GG, _O_MRG, _O_END) = [int(v) for v in _OFF]

PA_W = 3 * A_QK
PM_W = M_QKW * 2 + 2 * M_VW + LANES
PM_GATE_BLK = (M_QKW * 2 + 2 * M_VW) // LANES
PG_W = 4 * G_W + LANES
PG_GATE_BLK = 4 * G_W // LANES


def _params(sem):
    return pltpu.CompilerParams(dimension_semantics=sem, vmem_limit_bytes=VMEM_LIMIT)


def _sigmoid(x):
    return 1.0 / (1.0 + jnp.exp(-x))


def _softplus(x):
    return jnp.maximum(x, 0.0) + jnp.log1p(jnp.exp(-jnp.abs(x)))


def _log_sigmoid(x):
    return -_softplus(-x)


def _rms(x, gain):
    return x * lax.rsqrt(jnp.mean(x * x, axis=-1, keepdims=True) + EPS) * gain


def _dot(a, b):
    return jnp.dot(a.astype(BF16), b.astype(BF16), preferred_element_type=F32)


def _dot_nt(a, b):
    return lax.dot_general(a.astype(BF16), b.astype(BF16), (((1,), (1,)), ((), ())),
                           preferred_element_type=F32)


def _dot_tn(a, b):
    return lax.dot_general(a.astype(BF16), b.astype(BF16), (((0,), (0,)), ((), ())),
                           preferred_element_type=F32)


def _dot_exact(a, b):
    return jnp.dot(a, b, precision=lax.Precision.HIGHEST, preferred_element_type=F32)


def _ada_kernel(c_ref, w_ref, b_ref, o_ref):
    c = c_ref[...]
    s = c * _sigmoid(c)
    o_ref[...] = _dot_exact(s, w_ref[...]) + b_ref[...]


def _ada_all(cond, w_ada, b_ada):
    depth = w_ada.shape[0]
    rows = cond.shape[0]
    n = w_ada.shape[2]
    tn = 1024
    return pl.pallas_call(
        _ada_kernel,
        grid=(depth, n // tn),
        in_specs=[pl.BlockSpec((rows, D_MODEL), lambda l, j: (0, 0)),
                  pl.BlockSpec((None, D_MODEL, tn), lambda l, j: (l, 0, j)),
                  pl.BlockSpec((None, 1, tn), lambda l, j: (l, 0, j))],
        out_specs=pl.BlockSpec((None, rows, tn), lambda l, j: (l, 0, j)),
        out_shape=jax.ShapeDtypeStruct((depth, rows, n), F32),
        compiler_params=_params(("parallel", "parallel")),
        name="ada",
    )(cond, w_ada, b_ada.reshape(depth, 1, n))


def _rot_half(x):
    lane = lax.broadcasted_iota(jnp.int32, x.shape, 1)
    first = (lane % 32) < 16
    return jnp.where(first, -pltpu.roll(x, LANES - 16, 1), pltpu.roll(x, 16, 1))


def _proj_kernel(x_ref, mod_ref, g_ref, wa_ref, wm_ref, wg_ref, wr_ref, ba_ref, bm_ref, bg_ref, br_ref,
                 cos_ref, sin_ref, pa_ref, pm_ref, pg_ref, gs_ref, *, nct):
    x = x_ref[...]
    h = (_rms(x, g_ref[...]) * (1.0 + mod_ref[1:2, :]) + mod_ref[0:1, :]).astype(BF16)

    def project(w_ref, b_ref):
        return jnp.dot(h, w_ref[...], preferred_element_type=F32) + b_ref[...]

    acc = project(wa_ref, ba_ref)
    is_lat = pl.program_id(1) >= nct
    cos = jnp.where(is_lat, cos_ref[...], 1.0)
    sin = jnp.where(is_lat, sin_ref[...], 0.0)
    for j in range(2 * A_QK // LANES):
        blk = acc[:, j * LANES:(j + 1) * LANES]
        r = blk * cos + _rot_half(blk) * sin
        if j < A_QK // LANES:
            r = r * (A_HD ** -0.5)
        pa_ref[:, j * LANES:(j + 1) * LANES] = r.astype(pa_ref.dtype)
    pa_ref[:, 2 * A_QK:] = acc[:, 2 * A_QK:].astype(pa_ref.dtype)
    pm_ref[...] = project(wm_ref, bm_ref)
    pg_ref[...] = project(wg_ref, bg_ref)
    gs_ref[...] = _sigmoid(project(wr_ref, br_ref)).astype(gs_ref.dtype)


def _proj(x_all, mod, gain, weights, biases, rope, *, nct):
    bsz, t, _ = x_all.shape
    nt = t // TOKEN_TILE
    widths = [w.shape[1] for w in weights]
    const = lambda shape: pl.BlockSpec(shape, lambda i, j: (0, 0), pipeline_mode=pl.Buffered(1))
    rope_spec = pl.BlockSpec((TOKEN_TILE, LANES), lambda i, j: (jnp.maximum(j - nct, 0), 0))
    tile = lambda n: pl.BlockSpec((None, TOKEN_TILE, n), lambda i, j: (i, j, 0))
    return pl.pallas_call(
        functools.partial(_proj_kernel, nct=nct),
        grid=(bsz, nt),
        in_specs=[tile(D_MODEL),
                  pl.BlockSpec((None, 6, D_MODEL), lambda i, j: (2 * i + (j >= nct).astype(jnp.int32), 0, 0)),
                  const((1, D_MODEL))]
                 + [const((D_MODEL, n)) for n in widths] + [const((1, n)) for n in widths]
                 + [rope_spec, rope_spec],
        out_specs=[tile(n) for n in widths],
        out_shape=[jax.ShapeDtypeStruct((bsz, t, n), dt) for n, dt in zip(widths, (BF16, F32, F32, BF16))],
        compiler_params=_params(("parallel", "parallel")),
        name="proj",
    )(x_all, mod, gain, *weights, *biases, *rope)


def _attn_kernel(q_ref, k_ref, v_ref, lam_ref, an_ref, o_ref, *, nct, tc, t_all, lam_init):
    qi = pl.program_id(2)
    q = q_ref[...]
    lane = lax.broadcasted_iota(jnp.int32, q.shape, 1)
    zero = jnp.zeros_like(q)
    qm = (jnp.where(lane < A_HD, q, zero), jnp.where(lane >= A_HD, q, zero))
    lv = lam_ref[...].astype(F32)
    lam = (jnp.exp(jnp.sum(lv[0:1] * lv[1:2], axis=-1, keepdims=True))
           - jnp.exp(jnp.sum(lv[2:3] * lv[3:4], axis=-1, keepdims=True)) + lam_init)

    def attend(n_keys):
        pv = []
        for mp in range(2):
            s = lax.dot_general(qm[mp], k_ref[0:n_keys, :], (((1,), (1,)), ((), ())),
                                preferred_element_type=F32)
            p = jnp.exp(s - jnp.max(s, axis=-1, keepdims=True))
            l = jnp.sum(p, axis=-1, keepdims=True)
            pv.append(jnp.dot(p.astype(BF16), v_ref[0:n_keys, :], preferred_element_type=F32) / l)
        o = pv[0] - lam * pv[1]
        o_ref[...] = _rms(o, an_ref[...]) * (1.0 - lam_init)

    @pl.when(qi < nct)
    def _():
        attend(tc)

    @pl.when(qi >= nct)
    def _():
        attend(t_all)


def _attention(pa, a_lambda, a_norm, *, nct, tc, lam_init):
    bsz, t, _ = pa.shape
    nt = t // TOKEN_TILE
    kb = A_QK // LANES
    return pl.pallas_call(
        functools.partial(_attn_kernel, nct=nct, tc=tc, t_all=t, lam_init=lam_init),
        grid=(bsz, A_HEADS, nt),
        in_specs=[
            pl.BlockSpec((None, TOKEN_TILE, LANES), lambda b, h, i: (b, i, h)),
            pl.BlockSpec((None, t, LANES), lambda b, h, i: (b, 0, kb + h)),
            pl.BlockSpec((None, t, LANES), lambda b, h, i: (b, 0, 2 * kb + h)),
            pl.BlockSpec((4, A_HD), lambda b, h, i: (0, 0)),
            pl.BlockSpec((1, A_VD), lambda b, h, i: (0, 0)),
        ],
        out_specs=pl.BlockSpec((None, TOKEN_TILE, LANES), lambda b, h, i: (b, i, h)),
        out_shape=jax.ShapeDtypeStruct((bsz, t, A_VW), F32),
        compiler_params=_params(("parallel", "parallel", "arbitrary")),
        name="diff_attn",
    )(pa, pa, pa, a_lambda, a_norm.reshape(1, A_VD))


def _conv_kernel(x_ref, w_ref, o_ref, *, tc):
    j = pl.program_id(1)
    x = x_ref[...]
    t_all = x.shape[0]
    w = w_ref[...]
    row = lax.broadcasted_iota(jnp.int32, x.shape, 0)
    in_ctx = row < tc
    pos = jnp.where(in_ctx, row, row - tc)
    seg = jnp.where(in_ctx, tc, t_all - tc)
    half = CONV_K // 2
    acc = x * w[half:half + 1, :]
    for d in range(-half, half + 1):
        if d == 0:
            continue
        shifted = pltpu.roll(x, (-d) % t_all, 0)
        ok = (pos + d >= 0) & (pos + d < seg)
        acc = acc + jnp.where(ok, shifted, 0.0) * w[half + d:half + d + 1, :]
    y = acc * _sigmoid(acc)
    nrm = y * lax.rsqrt(jnp.sum(y * y, axis=-1, keepdims=True) + EPS)
    nq = G_W // LANES
    o_ref[...] = jnp.where(j < nq, nrm * (G_HD ** -0.5), jnp.where(j < 2 * nq, nrm, y))


def _gdn_conv(pg, conv_w, *, tc):
    bsz, t, _ = pg.shape
    nblk = 3 * G_W // LANES
    return pl.pallas_call(
        functools.partial(_conv_kernel, tc=tc),
        grid=(bsz, nblk),
        in_specs=[pl.BlockSpec((None, t, LANES), lambda b, j: (b, 0, j)),
                  pl.BlockSpec((CONV_K, LANES), lambda b, j: (0, j))],
        out_specs=pl.BlockSpec((None, t, LANES), lambda b, j: (b, 0, j)),
        out_shape=jax.ShapeDtypeStruct((bsz, t, 3 * G_W), F32),
        compiler_params=_params(("parallel", "parallel")),
        name="gdn_conv",
    )(pg, conv_w)


def _bwd_tile(s, nct, nt):
    return jnp.where(s < nct, nct - 1 - s, nt - 1 - (s - nct))


def _split2(x):
    hi = x.astype(BF16)
    return hi, (x - hi.astype(F32)).astype(BF16)


def _split3(x):
    hi = x.astype(BF16)
    r = x - hi.astype(F32)
    mid = r.astype(BF16)
    return hi, mid, (r - mid.astype(F32)).astype(BF16)


def _dot_f32ish(a2, b2):
    (ah, al), (bh, bl) = a2, b2
    return (jnp.dot(ah, bl, preferred_element_type=F32) + jnp.dot(al, bh, preferred_element_type=F32)
            + jnp.dot(ah, bh, preferred_element_type=F32))


def _sel_dot(sel, x):
    hi, mid, lo = _split3(x)
    return (jnp.dot(sel, lo, preferred_element_type=F32) + jnp.dot(sel, mid, preferred_element_type=F32)
            + jnp.dot(sel, hi, preferred_element_type=F32))


def _dot_sel(x, sel):
    hi, mid, lo = _split3(x)
    return (jnp.dot(lo, sel, preferred_element_type=F32) + jnp.dot(mid, sel, preferred_element_type=F32)
            + jnp.dot(hi, sel, preferred_element_type=F32))


def _recurrent_call(kernel_fn, name, qkv, qkv_w, gate_src, gate_blk, gates_t, small, out_w, scratch, *, nct):
    bsz, t, _ = qkv.shape
    nt = t // TOKEN_TILE
    fwd = lambda b, s: (b, s, 0)
    bwd = lambda b, s: (b, _bwd_tile(s, nct, nt), 0)
    fwd_g = lambda b, s: (b, s, gate_blk)
    bwd_g = lambda b, s: (b, _bwd_tile(s, nct, nt), gate_blk)
    fwd_t = lambda b, s: (b, 0, s)
    bwd_t = lambda b, s: (b, 0, _bwd_tile(s, nct, nt))
    out = jax.ShapeDtypeStruct((bsz, t, out_w), F32)
    return pl.pallas_call(
        kernel_fn,
        grid=(bsz, nt),
        in_specs=[
            pl.BlockSpec((None, TOKEN_TILE, qkv_w), fwd),
            pl.BlockSpec((None, TOKEN_TILE, LANES), fwd_g),
            pl.BlockSpec((None, N_GATES, TOKEN_TILE), fwd_t),
            pl.BlockSpec((None, TOKEN_TILE, qkv_w), bwd),
            pl.BlockSpec((None, TOKEN_TILE, LANES), bwd_g),
            pl.BlockSpec((None, N_GATES, TOKEN_TILE), bwd_t),
        ] + [pl.BlockSpec(a.shape, lambda b, s: (0, 0)) for a in small],
        out_specs=[pl.BlockSpec((None, TOKEN_TILE, out_w), fwd),
                   pl.BlockSpec((None, TOKEN_TILE, out_w), bwd)],
        out_shape=[out, out],
        scratch_shapes=scratch,
        compiler_params=_params(("parallel", "arbitrary")),
        name=name,
    )(qkv, gate_src, gates_t, qkv, gate_src, gates_t, *small)


def _mlstm_kernel(qkv_f, g_f, gt_f, qkv_b, g_b, gt_b, fbr_ref, fbc_ref, hf_ref, hb_ref,
                  c_scr, n_scr, m_scr):
    @pl.when(pl.program_id(1) == 0)
    def _():
        c_scr[...] = jnp.zeros_like(c_scr)
        n_scr[...] = jnp.zeros_like(n_scr)
        m_scr[...] = jnp.zeros_like(m_scr)

    n = TOKEN_TILE
    row = lax.broadcasted_iota(jnp.int32, (n, n), 0)
    col = lax.broadcasted_iota(jnp.int32, (n, n), 1)
    lower, upper = col <= row, col >= row
    lower_s, upper_s = lower.astype(BF16), upper.astype(BF16)
    fbr = fbr_ref[...]
    fbc = fbc_ref[...]
    dirs = ((qkv_f, g_f, gt_f, hf_ref, lower, lower_s, upper_s),
            (qkv_b, g_b, gt_b, hb_ref, upper, upper_s, lower_s))

    probs = []
    for di, (qkv, g_ref, gt_ref, h_ref, incl, csum, csum_t) in enumerate(dirs):
        g = g_ref[:, 0:N_GATES]
        gt = gt_ref[...]
        ls = _log_sigmoid(g + fbr)
        lst = _log_sigmoid(gt + fbc)
        bc_all = _sel_dot(csum, ls)
        br_all = _dot_sel(lst, csum_t)
        tot = jnp.sum(ls, axis=0, keepdims=True)
        for h in range(M_HEADS):
            ic = di * 8 + h
            fc = di * 8 + 4 + h
            base = h * 2 * M_QK
            st = di * M_HEADS + h
            probs.append(dict(
                st=st, h=h, h_ref=h_ref, incl=incl,
                q=qkv[:, base:base + M_QK],
                k=qkv[:, base + M_QK:base + 2 * M_QK] * (M_QK ** -0.5),
                v=qkv[:, 2 * M_QKW + h * M_V:2 * M_QKW + (h + 1) * M_V],
                i_c=g[:, ic:ic + 1], b_c=bc_all[:, fc:fc + 1], i_r=gt[ic:ic + 1, :],
                b_r=br_all[fc:fc + 1, :], btot=tot[:, fc:fc + 1],
                c=c_scr[st], n=n_scr[st], m=m_scr[st]))
    for p in probs:
        p["qk"] = _dot_nt(p["q"], p["k"])
        p["qc"] = _dot(p["q"], p["c"])
    for p in probs:
        d = jnp.where(p["incl"], p["b_c"] - p["b_r"] + p["i_r"], -jnp.inf)
        inter = p["b_c"] + p["m"]
        p["m_row"] = jnp.maximum(inter, jnp.max(d, axis=-1, keepdims=True))
        p["e_inter"] = jnp.exp(inter - p["m_row"])
        p["s"] = p["qk"] * jnp.exp(d - p["m_row"])
    for p in probs:
        a_r = p["btot"] - p["b_r"] + p["i_r"]
        a_c = p["btot"] - p["b_c"] + p["i_c"]
        m_new = jnp.maximum(p["btot"] + p["m"], jnp.max(a_r, axis=-1, keepdims=True))
        p["carry_w"] = jnp.exp(p["btot"] + p["m"] - m_new)
        p["kw"] = p["k"] * jnp.exp(a_c - m_new)
        m_scr[p["st"]] = m_new
    for p in probs:
        num = _dot(p["s"], p["v"]) + p["e_inter"] * p["qc"]
        den = (jnp.sum(p["s"], axis=-1, keepdims=True)
               + p["e_inter"] * jnp.sum(p["q"] * p["n"], axis=-1, keepdims=True))
        hh = num / jnp.maximum(jnp.abs(den), jnp.exp(-p["m_row"]))
        p["h_ref"][:, p["h"] * M_V:(p["h"] + 1) * M_V] = hh
    for p in probs:
        c_scr[p["st"]] = p["carry_w"] * p["c"] + _dot_tn(p["kw"], p["v"])
        n_scr[p["st"]] = p["carry_w"] * p["n"] + jnp.sum(p["kw"], axis=0, keepdims=True)


def _mlstm(pm, gates_t, fb_row, fb_col, *, nct):
    scratch = [pltpu.VMEM((2 * M_HEADS, M_QK, M_V), F32),
               pltpu.VMEM((2 * M_HEADS, 1, M_QK), F32),
               pltpu.VMEM((2 * M_HEADS, 1, 1), F32)]
    return _recurrent_call(_mlstm_kernel, "mlstm", pm, 2 * M_QKW + M_VW, pm, PM_GATE_BLK, gates_t,
                           [fb_row, fb_col], M_VW, scratch, nct=nct)


def _unit_tri_inverse_many(a_list, eye):
    n_iter = int(math.log2(CHUNK)) - 1
    xs = [(eye - a).astype(BF16) for a in a_list]
    ps = [a.astype(BF16) for a in a_list]
    for it in range(n_iter):
        ps = [jnp.dot(p, p, preferred_element_type=F32).astype(BF16) for p in ps]
        xs = [x.astype(F32) + jnp.dot(x, p, preferred_element_type=F32) for x, p in zip(xs, ps)]
        if it < n_iter - 1:
            xs = [x.astype(BF16) for x in xs]
    out = []
    for a, x in zip(a_list, xs):
        x2 = _split2(x)
        r = (eye - x) - _dot_f32ish(_split2(a), x2)
        out.append(x + jnp.dot(x2[0], r.astype(BF16), preferred_element_type=F32))
    return out


def _gdn_kernel(qkv_f, g_f, gt_f, qkv_b, g_b, gt_b, par_r, par_c, of_ref, ob_ref, s_scr):
    @pl.when(pl.program_id(1) == 0)
    def _():
        s_scr[...] = jnp.zeros_like(s_scr)

    n = TOKEN_TILE
    nch = n // CHUNK
    row = lax.broadcasted_iota(jnp.int32, (n, n), 0)
    col = lax.broadcasted_iota(jnp.int32, (n, n), 1)
    same = (row // CHUNK) == (col // CHUNK)
    lower, upper = same & (col <= row), same & (col >= row)
    lower_x, upper_x = same & (col < row), same & (col > row)
    lower_s, upper_s, same_s = lower.astype(BF16), upper.astype(BF16), same.astype(BF16)
    eye = jnp.where(row == col, 1.0, 0.0)
    alog_r, dtb_r = par_r[0:1, :], par_r[1:2, :]
    alog_c, dtb_c = par_c[:, 0:1], par_c[:, 1:2]
    col_r = lax.broadcasted_iota(jnp.int32, (n, N_GATES), 1)
    row_c = lax.broadcasted_iota(jnp.int32, (N_GATES, n), 0)
    dirs = ((qkv_f, g_f, gt_f, of_ref, lower, lower_x, lower_s, upper_s),
            (qkv_b, g_b, gt_b, ob_ref, upper, upper_x, upper_s, lower_s))

    probs = []
    for di, (qkv, g_ref, gt_ref, o_ref, incl, strict, csum, csum_t) in enumerate(dirs):
        g = g_ref[:, 0:N_GATES]
        gt = gt_ref[...]
        val = jnp.where(col_r < 8, -jnp.exp(alog_r) * _softplus(g + dtb_r), _sigmoid(g))
        val_t = jnp.where(row_c < 8, -jnp.exp(alog_c) * _softplus(gt + dtb_c), _sigmoid(gt))
        gc_all = _sel_dot(csum, val)
        gr_all = _dot_sel(val_t, csum_t)
        ge_all = _sel_dot(same_s, val)
        for h in range(G_HEADS):
            ac = di * 4 + h
            bc = 8 + di * 4 + h
            q = qkv[:, h * G_HD:(h + 1) * G_HD]
            k = qkv[:, G_W + h * G_HD:G_W + (h + 1) * G_HD]
            v = qkv[:, 2 * G_W + h * G_HD:2 * G_W + (h + 1) * G_HD]
            gc_c, gc_r = gc_all[:, ac:ac + 1], gr_all[ac:ac + 1, :]
            beta_c, ge_c = val[:, bc:bc + 1], ge_all[:, ac:ac + 1]
            decay = jnp.where(incl, jnp.exp(jnp.where(incl, gc_c - gc_r, 0.0)), 0.0)
            kb = k * beta_c
            eg = jnp.exp(gc_c)
            probs.append(dict(
                di=di, h=h, o_ref=o_ref,
                a=jnp.where(strict, _dot_nt(kb, k) * decay, 0.0),
                rhs=jnp.concatenate([v * beta_c, kb * eg], axis=-1),
                k_end=k * jnp.exp(ge_c - gc_c),
                qk=jnp.where(incl, _dot_nt(q, k) * decay, 0.0),
                qg=q * eg,
                carry=jnp.exp(ge_c),
                s=s_scr[di * G_HEADS + h]))
    t_inv = _unit_tri_inverse_many([p["a"] for p in probs], eye)
    for p, t in zip(probs, t_inv):
        uw = _dot(t, p["rhs"])
        p["u"], p["w"] = uw[:, 0:G_HD], uw[:, G_HD:]
        p["v_new"] = [None] * nch
        p["o_inter"] = [None] * nch

    for ci in range(nch):
        rows = [slice(cc * CHUNK, (cc + 1) * CHUNK)
                for cc in [ci if p["di"] == 0 else nch - 1 - ci for p in probs]]
        s_bf = [p["s"].astype(BF16) for p in probs]
        v_new = [p["u"][r] - jnp.dot(p["w"][r].astype(BF16), sb, preferred_element_type=F32)
                 for p, r, sb in zip(probs, rows, s_bf)]
        upd = [_dot_tn(p["k_end"][r], vn) for p, r, vn in zip(probs, rows, v_new)]
        for p, r, sb, vn, up in zip(probs, rows, s_bf, v_new, upd):
            cc = r.start // CHUNK
            p["o_inter"][cc] = jnp.dot(p["qg"][r].astype(BF16), sb, preferred_element_type=F32)
            p["v_new"][cc] = vn
            p["s"] = p["s"] * p["carry"][r.start:r.start + 1, :] + up
    for p in probs:
        o = jnp.concatenate(p["o_inter"], axis=0) + _dot(p["qk"], jnp.concatenate(p["v_new"], axis=0))
        s_scr[p["di"] * G_HEADS + p["h"]] = p["s"]
        p["o_ref"][:, p["h"] * G_HD:(p["h"] + 1) * G_HD] = o


def _gdn(qkv, pg, gates_t, par_row, par_col, *, nct):
    scratch = [pltpu.VMEM((2 * G_HEADS, G_HD, G_HD), F32)]
    return _recurrent_call(_gdn_kernel, "gdn", qkv, 3 * G_W, pg, PG_GATE_BLK, gates_t,
                           [par_row, par_col], G_W, scratch, nct=nct)


def _post_kernel(x_ref, mod_ref, gain_ref, ya_ref, hf_ref, hb_ref, o_ref, of_ref, ob_ref, z_ref,
                 gs_ref, mn_ref, gn_ref, wb_ref, wo_ref, w1_ref, w2_ref, out_ref):
    hs = hf_ref[...] + hb_ref[...]
    og = o_ref[...]
    os_ = of_ref[...] + ob_ref[...]
    z = z_ref[...]
    ym, yg = [], []
    for h in range(M_HEADS):
        sl = slice(h * M_V, (h + 1) * M_V)
        ym.append(_rms(hs[:, sl], mn_ref[:, sl]) * _sigmoid(og[:, sl]))
    for h in range(G_HEADS):
        sl = slice(h * G_HD, (h + 1) * G_HD)
        zz = z[:, sl]
        yg.append(_rms(os_[:, sl], gn_ref[...]) * (zz * _sigmoid(zz)))
    ym = jnp.concatenate(ym, axis=-1)
    yg = jnp.concatenate(yg, axis=-1)
    y = (gs_ref[:, 0:D_MODEL].astype(F32) * _dot(ya_ref[...], wb_ref[0])
         + gs_ref[:, D_MODEL:2 * D_MODEL].astype(F32) * _dot(ym, wb_ref[1])
         + gs_ref[:, 2 * D_MODEL:].astype(F32) * _dot(yg, wb_ref[2]))
    y2 = _dot(y, wo_ref[...])
    x = x_ref[...] + mod_ref[2:3, :] * _rms(y2, gain_ref[1:2, :])

    h = (_rms(x, gain_ref[2:3, :]) * (1.0 + mod_ref[4:5, :]) + mod_ref[3:4, :]).astype(BF16)
    acc = jnp.zeros((x.shape[0], D_MODEL), F32)
    step = 1024
    for j in range(D_FF // step):
        a = jnp.maximum(jnp.dot(h, w1_ref[:, j * step:(j + 1) * step], preferred_element_type=F32), 0.0)
        acc = acc + jnp.dot((a * a).astype(BF16), w2_ref[j * step:(j + 1) * step, :], preferred_element_type=F32)
    out_ref[...] = x + mod_ref[5:6, :] * _rms(acc, gain_ref[3:4, :])


def _post(x_all, mod, gains, ya, hf, hb, pm, of, ob, pg, gsig, m_norm, g_norm, wb, wo, w1, w2, *, nct, t_off):
    bsz, t, _ = x_all.shape
    nt = t // TOKEN_TILE - t_off
    tile = lambda w, blk=0: pl.BlockSpec((None, TOKEN_TILE, w), lambda b, i: (b, i + t_off, blk))
    const = lambda shape: pl.BlockSpec(shape, lambda b, i: (0,) * len(shape), pipeline_mode=pl.Buffered(1))
    return pl.pallas_call(
        _post_kernel,
        grid=(bsz, nt),
        in_specs=[
            tile(D_MODEL),
            pl.BlockSpec((None, 6, D_MODEL), lambda b, i: (2 * b + (i + t_off >= nct).astype(jnp.int32), 0, 0)),
            const((4, D_MODEL)),
            tile(A_VW), tile(M_VW), tile(M_VW),
            tile(M_VW, (2 * M_QKW + M_VW) // M_VW),
            tile(G_W), tile(G_W),
            tile(G_W, 3),
            tile(3 * D_MODEL),
            const((1, M_VW)), const((1, G_HD)),
            const((3, BRANCH_W, D_MODEL)), const((D_MODEL, D_MODEL)),
            const((D_MODEL, D_FF)), const((D_FF, D_MODEL)),
        ],
        out_specs=pl.BlockSpec((None, TOKEN_TILE, D_MODEL), lambda b, i: (b, i, 0)),
        out_shape=jax.ShapeDtypeStruct((bsz, nt * TOKEN_TILE, D_MODEL), F32),
        compiler_params=_params(("parallel", "parallel")),
        name="post",
    )(x_all, mod, gains, ya, hf, hb, pm, of, ob, pg, gsig, m_norm, g_norm, wb, wo, w1, w2)


def _rope_tables(n_lat):
    rows = n_lat // GRID_W
    row = jnp.repeat(jnp.arange(rows, dtype=F32), GRID_W)
    col = jnp.tile(jnp.arange(GRID_W, dtype=F32), rows)
    half = A_HD // 2
    inv_freq = ROPE_BASE ** (-jnp.arange(0, half, 2, dtype=F32) / half)
    ang_r = row[:, None] * inv_freq
    ang_c = col[:, None] * inv_freq
    ang = jnp.concatenate([ang_r, ang_r, ang_c, ang_c] * 2, axis=-1)
    return jnp.cos(ang), jnp.sin(ang)


def _pad_cols(a, width):
    return jnp.pad(a, [(0, 0)] * (a.ndim - 1) + [(0, width - a.shape[-1])])


def _split_in_proj(w, b):
    def both(f):
        return f(w).astype(BF16), f(b[None, :])

    qk_idx = np.concatenate([np.concatenate([np.arange(_O_MQ + h * M_QK, _O_MQ + (h + 1) * M_QK),
                                             np.arange(_O_MK + h * M_QK, _O_MK + (h + 1) * M_QK)])
                             for h in range(M_HEADS)])
    attn = both(lambda a: a[:, _O_AQ:_O_MQ])
    mlstm = both(lambda a: jnp.concatenate(
        [a[:, qk_idx], a[:, _O_MV:_O_MG], _pad_cols(a[:, _O_MG:_O_GQ], LANES)], axis=-1))
    gdn = both(lambda a: jnp.concatenate(
        [a[:, _O_GQ:_O_GG], _pad_cols(a[:, _O_GG:_O_MRG], LANES)], axis=-1))
    mrg = both(lambda a: a[:, _O_MRG:_O_END])
    return attn, mlstm, gdn, mrg


def kernel(x, c, ctx, c_ctx, w_ada, b_ada, norm_gains, w_in, b_in, a_lambda, a_norm, m_fbias, m_norm,
           g_conv, g_alog, g_dtbias, g_norm, w_branch, w_out, w_ff1, w_ff2):
    bsz, n_lat, _ = x.shape
    tc = ctx.shape[1]
    depth = w_ada.shape[0]
    assert tc % TOKEN_TILE == 0 and n_lat % TOKEN_TILE == 0 and TOKEN_TILE % CHUNK == 0
    nct = tc // TOKEN_TILE

    rows = -(-(bsz + 1) // 8) * 8
    cond = jnp.zeros((rows, D_MODEL), F32).at[:bsz].set(c).at[bsz].set(c_ctx)
    mod_all = _ada_all(cond, w_ada, b_ada)
    ctx_rows = jnp.broadcast_to(mod_all[:, bsz:bsz + 1], (depth, bsz, 6 * D_MODEL))
    mod_all = jnp.stack([ctx_rows, mod_all[:, :bsz]], axis=2).reshape(depth, 2 * bsz, 6, D_MODEL)

    rope = _rope_tables(n_lat)
    x_all = jnp.concatenate([ctx, x], axis=1)
    zeros4 = jnp.zeros((4,), F32)

    for i in range(depth):
        lam_init = 0.8 - 0.6 * math.exp(-0.3 * i)
        last = i == depth - 1
        mod = mod_all[i]
        gains = norm_gains[i]
        (wa, ba), (wm, bm), (wg, bg), (wr, br) = _split_in_proj(w_in[i], b_in[i])
        g0 = gains[0:1]

        pa, pm, pg, gsig = _proj(x_all, mod, g0, (wa, wm, wg, wr), (ba, bm, bg, br), rope, nct=nct)

        ya = _attention(pa, a_lambda[i], a_norm[i], nct=nct, tc=tc, lam_init=lam_init)

        fb = jnp.concatenate([zeros4, m_fbias[i, 0], zeros4, m_fbias[i, 1]])
        m_gt = jnp.swapaxes(pm[:, :, PM_GATE_BLK * LANES:PM_GATE_BLK * LANES + N_GATES], 1, 2)
        hf, hb = _mlstm(pm, m_gt, fb[None, :], fb[:, None], nct=nct)

        qkv = _gdn_conv(pg, g_conv[i], tc=tc)
        alog = jnp.concatenate([g_alog[i].reshape(-1), jnp.zeros((8,), F32)])
        dtb = jnp.concatenate([g_dtbias[i].reshape(-1), jnp.zeros((8,), F32)])
        par = jnp.stack([alog, dtb])
        g_gt = jnp.swapaxes(pg[:, :, PG_GATE_BLK * LANES:PG_GATE_BLK * LANES + N_GATES], 1, 2)
        of, ob = _gdn(qkv, pg, g_gt, par, par.T, nct=nct)

        t_off = nct if last else 0
        x_all = _post(x_all, mod, gains, ya, hf, hb, pm, of, ob, pg, gsig,
                      m_norm[i].reshape(1, M_VW), g_norm[i].reshape(1, G_HD),
                      w_branch[i].astype(BF16), w_out[i].astype(BF16),
                      w_ff1[i].astype(BF16), w_ff2[i].astype(BF16), nct=nct, t_off=t_off)
    return x_all
```

```python
import functools
import math

import jax
import jax.numpy as jnp
import numpy as np
from jax import lax
from jax.experimental import pallas as pl
from jax.experimental.pallas import tpu as pltpu

F32 = jnp.float32
BF16 = jnp.bfloat16

D_MODEL = 1024
GRID_W = 64
A_HEADS = 4
A_HD = 64
A_VD = 2 * A_HD
A_QK = A_HEADS * 2 * A_HD
A_VW = A_HEADS * A_VD
M_HEADS = 4
M_QK = 64
M_V = 128
M_QKW = M_HEADS * M_QK
M_VW = M_HEADS * M_V
G_HEADS = 4
G_HD = 128
G_W = G_HEADS * G_HD
CONV_K = 5
BRANCH_W = 512
D_FF = 4 * D_MODEL
ROPE_BASE = 10000.0
EPS = 1e-6
LOG2_E = math.log2(math.e)
N_GATES = 16

LANES = 128
TOKEN_TILE = 256
CHUNK = 64
VMEM_LIMIT = 56 * 1024 * 1024

_OFF = np.cumsum([0, A_QK, A_QK, A_VW, M_QKW, M_QKW, M_VW, M_VW, 4 * M_HEADS,
                  G_W, G_W, G_W, G_W, 4 * G_HEADS, 3 * D_MODEL])
(_O_AQ, _O_AK, _O_AV, _O_MQ, _O_MK, _O_MV, _O_MO, _O_MG,
 _O_GQ, _O_GK, _O_GV, _O_GZ, _O_GG, _O_MRG, _O_END) = [int(v) for v in _OFF]

PA_W = 3 * A_QK
PM_W = M_QKW * 2 + 2 * M_VW + LANES
PM_GATE_BLK = (M_QKW * 2 + 2 * M_VW) // LANES
PG_W = G_W + LANES
PG_GATE_BLK = G_W // LANES
HALO = 8


def _params(sem):
    return pltpu.CompilerParams(dimension_semantics=sem, vmem_limit_bytes=VMEM_LIMIT)


def _sigmoid(x):
    return 1.0 / (1.0 + jnp.exp(-x))


def _softplus(x):
    return jnp.maximum(x, 0.0) + jnp.log1p(jnp.exp(-jnp.abs(x)))


def _log_sigmoid(x):
    return -_softplus(-x)


def _rms(x, gain):
    return x * lax.rsqrt(jnp.mean(x * x, axis=-1, keepdims=True) + EPS) * gain


def _dot(a, b):
    return jnp.dot(a.astype(BF16), b.astype(BF16), preferred_element_type=F32)


def _dot_nt(a, b):
    return lax.dot_general(a.astype(BF16), b.astype(BF16), (((1,), (1,)), ((), ())),
                           preferred_element_type=F32)


def _dot_tn(a, b):
    return lax.dot_general(a.astype(BF16), b.astype(BF16), (((0,), (0,)), ((), ())),
                           preferred_element_type=F32)


def _dot_exact(a, b):
    return jnp.dot(a, b, precision=lax.Precision.HIGHEST, preferred_element_type=F32)


def _ada_kernel(c_ref, w_ref, b_ref, o_ref):
    c = c_ref[...]
    s = c * _sigmoid(c)
    o_ref[...] = _dot_exact(s, w_ref[...]) + b_ref[...]


def _ada_all(cond, w_ada, b_ada):
    depth = w_ada.shape[0]
    rows = cond.shape[0]
    n = w_ada.shape[2]
    tn = 1024
    return pl.pallas_call(
        _ada_kernel,
        grid=(depth, n // tn),
        in_specs=[pl.BlockSpec((rows, D_MODEL), lambda l, j: (0, 0)),
                  pl.BlockSpec((None, D_MODEL, tn), lambda l, j: (l, 0, j)),
                  pl.BlockSpec((None, 1, tn), lambda l, j: (l, 0, j))],
        out_specs=pl.BlockSpec((None, rows, tn), lambda l, j: (l, 0, j)),
        out_shape=jax.ShapeDtypeStruct((depth, rows, n), F32),
        compiler_params=_params(("parallel", "parallel")),
        name="ada",
    )(cond, w_ada, b_ada.reshape(depth, 1, n))


def _rot_half(x):
    lane = lax.broadcasted_iota(jnp.int32, x.shape, 1)
    first = (lane % 32) < 16
    return jnp.where(first, -pltpu.roll(x, LANES - 16, 1), pltpu.roll(x, 16, 1))


def _proj_kernel(x_ref, xp_ref, xn_ref, mod_ref, g_ref, wa_ref, wm_ref, wq_ref, wg_ref, wr_ref,
                 ba_ref, bm_ref, bq_ref, bg_ref, br_ref, cw_ref, cos_ref, sin_ref,
                 pa_ref, pm_ref, qkv_ref, pg_ref, gs_ref, *, nct, nt):
    tile = pl.program_id(1)
    x = x_ref[...]
    xe = jnp.concatenate([xp_ref[...], x, xn_ref[...]], axis=0)
    he = (_rms(xe, g_ref[...]) * (1.0 + mod_ref[1:2, :]) + mod_ref[0:1, :]).astype(BF16)
    h = he[HALO:HALO + TOKEN_TILE]

    def project(w_ref, b_ref, lhs=h):
        return jnp.dot(lhs, w_ref[...], preferred_element_type=F32) + b_ref[...]

    ext = project(wq_ref, bq_ref, he)
    n_ext = ext.shape[0]
    first = (tile == 0) | (tile == nct)
    last = (tile == nct - 1) | (tile == nt - 1)
    row = lax.broadcasted_iota(jnp.int32, (TOKEN_TILE, 1), 0)
    half = CONV_K // 2
    cw = cw_ref[...]
    conv = ext[HALO:HALO + TOKEN_TILE] * cw[half:half + 1, :]
    for d in range(-half, half + 1):
        if d == 0:
            continue
        shifted = pltpu.roll(ext, (-d) % n_ext, 0)[HALO:HALO + TOKEN_TILE]
        bad = ((row + d < 0) & first) | ((row + d >= TOKEN_TILE) & last)
        conv = conv + jnp.where(bad, 0.0, shifted) * cw[half + d:half + d + 1, :]
    y = conv * _sigmoid(conv)
    for j in range(3 * G_HEADS):
        sl = slice(j * G_HD, (j + 1) * G_HD)
        yh = y[:, sl]
        if j < 2 * G_HEADS:
            yh = yh * lax.rsqrt(jnp.sum(yh * yh, axis=-1, keepdims=True) + EPS)
        if j < G_HEADS:
            yh = yh * (G_HD ** -0.5)
        qkv_ref[:, sl] = yh

    acc = project(wa_ref, ba_ref)
    is_lat = tile >= nct
    cos = jnp.where(is_lat, cos_ref[...], 1.0)
    sin = jnp.where(is_lat, sin_ref[...], 0.0)
    for j in range(2 * A_QK // LANES):
        blk = acc[:, j * LANES:(j + 1) * LANES]
        r = blk * cos + _rot_half(blk) * sin
        if j < A_QK // LANES:
            r = r * (A_HD ** -0.5 * LOG2_E)
        pa_ref[:, j * LANES:(j + 1) * LANES] = r.astype(pa_ref.dtype)
    pa_ref[:, 2 * A_QK:] = acc[:, 2 * A_QK:].astype(pa_ref.dtype)
    pm_ref[...] = project(wm_ref, bm_ref)
    pg_ref[...] = project(wg_ref, bg_ref)
    gs_ref[...] = _sigmoid(project(wr_ref, br_ref)).astype(gs_ref.dtype)


def _proj(x_all, mod, gain, weights, biases, conv_w, rope, *, nct):
    bsz, t, _ = x_all.shape
    nt = t // TOKEN_TILE
    per_tile = TOKEN_TILE // HALO
    widths = [w.shape[1] for w in weights]
    const = lambda shape: pl.BlockSpec(shape, lambda i, j: (0, 0), pipeline_mode=pl.Buffered(1))
    rope_spec = pl.BlockSpec((TOKEN_TILE, LANES), lambda i, j: (jnp.maximum(j - nct, 0), 0))
    tile = lambda n: pl.BlockSpec((None, TOKEN_TILE, n), lambda i, j: (i, j, 0))
    prev_rows = pl.BlockSpec((None, HALO, D_MODEL), lambda i, j: (i, jnp.maximum(j * per_tile - 1, 0), 0))
    next_rows = pl.BlockSpec((None, HALO, D_MODEL),
                             lambda i, j: (i, jnp.minimum((j + 1) * per_tile, nt * per_tile - 1), 0))
    return pl.pallas_call(
        functools.partial(_proj_kernel, nct=nct, nt=nt),
        grid=(bsz, nt),
        in_specs=[tile(D_MODEL), prev_rows, next_rows,
                  pl.BlockSpec((None, 6, D_MODEL), lambda i, j: (2 * i + (j >= nct).astype(jnp.int32), 0, 0)),
                  const((1, D_MODEL))]
                 + [const((D_MODEL, n)) for n in widths] + [const((1, n)) for n in widths]
                 + [const(conv_w.shape), rope_spec, rope_spec],
        out_specs=[tile(n) for n in widths],
        out_shape=[jax.ShapeDtypeStruct((bsz, t, n), dt) for n, dt in zip(widths, (BF16, F32, F32, F32, BF16))],
        compiler_params=_params(("parallel", "parallel")),
        name="proj",
    )(x_all, x_all, x_all, mod, gain, *weights, *biases, conv_w, *rope)


def _attn_kernel(q_ref, k_ref, v_ref, lam_ref, an_ref, o_ref, *, nct, tc, t_all, lam_init):
    qi = pl.program_id(2)
    q = q_ref[...]
    lane = lax.broadcasted_iota(jnp.int32, q.shape, 1)
    zero = jnp.zeros_like(q)
    qm = (jnp.where(lane < A_HD, q, zero), jnp.where(lane >= A_HD, q, zero))
    lv = lam_ref[...].astype(F32)
    lam = (jnp.exp(jnp.sum(lv[0:1] * lv[1:2], axis=-1, keepdims=True))
           - jnp.exp(jnp.sum(lv[2:3] * lv[3:4], axis=-1, keepdims=True)) + lam_init)

    def attend(n_keys):
        pv = []
        for mp in range(2):
            s = lax.dot_general(qm[mp], k_ref[0:n_keys, :], (((1,), (1,)), ((), ())),
                                preferred_element_type=F32)
            p = jnp.exp2(s - jnp.max(s, axis=-1, keepdims=True))
            l = jnp.sum(p, axis=-1, keepdims=True)
            pv.append(jnp.dot(p.astype(BF16), v_ref[0:n_keys, :], preferred_element_type=F32) / l)
        o = pv[0] - lam * pv[1]
        o_ref[...] = _rms(o, an_ref[...]) * (1.0 - lam_init)

    @pl.when(qi < nct)
    def _():
        attend(tc)

    @pl.when(qi >= nct)
    def _():
        attend(t_all)


def _attention(pa, a_lambda, a_norm, *, nct, tc, lam_init):
    bsz, t, _ = pa.shape
    nt = t // TOKEN_TILE
    kb = A_QK // LANES
    return pl.pallas_call(
        functools.partial(_attn_kernel, nct=nct, tc=tc, t_all=t, lam_init=lam_init),
        grid=(bsz, A_HEADS, nt),
        in_specs=[
            pl.BlockSpec((None, TOKEN_TILE, LANES), lambda b, h, i: (b, i, h)),
            pl.BlockSpec((None, t, LANES), lambda b, h, i: (b, 0, kb + h)),
            pl.BlockSpec((None, t, LANES), lambda b, h, i: (b, 0, 2 * kb + h)),
            pl.BlockSpec((4, A_HD), lambda b, h, i: (0, 0)),
            pl.BlockSpec((1, A_VD), lambda b, h, i: (0, 0)),
        ],
        out_specs=pl.BlockSpec((None, TOKEN_TILE, LANES), lambda b, h, i: (b, i, h)),
        out_shape=jax.ShapeDtypeStruct((bsz, t, A_VW), F32),
        compiler_params=_params(("parallel", "parallel", "arbitrary")),
        name="diff_attn",
    )(pa, pa, pa, a_lambda, a_norm.reshape(1, A_VD))


def _bwd_tile(s, nct, nt):
    return jnp.where(s < nct, nct - 1 - s, nt - 1 - (s - nct))


def _split2(x):
    hi = x.astype(BF16)
    return hi, (x - hi.astype(F32)).astype(BF16)


def _split3(x):
    hi = x.astype(BF16)
    r = x - hi.astype(F32)
    mid = r.astype(BF16)
    return hi, mid, (r - mid.astype(F32)).astype(BF16)


def _dot_f32ish(a2, b2):
    (ah, al), (bh, bl) = a2, b2
    return (jnp.dot(ah, bl, preferred_element_type=F32) + jnp.dot(al, bh, preferred_element_type=F32)
            + jnp.dot(ah, bh, preferred_element_type=F32))


def _sel_dot(sel, x):
    hi, mid, lo = _split3(x)
    return (jnp.dot(sel, lo, preferred_element_type=F32) + jnp.dot(sel, mid, preferred_element_type=F32)
            + jnp.dot(sel, hi, preferred_element_type=F32))


def _dot_sel(x, sel):
    hi, mid, lo = _split3(x)
    return (jnp.dot(lo, sel, preferred_element_type=F32) + jnp.dot(mid, sel, preferred_element_type=F32)
            + jnp.dot(hi, sel, preferred_element_type=F32))


def _recurrent_call(kernel_fn, name, qkv, qkv_w, gate_src, gate_blk, gates_t, small, out_w, scratch, *, nct):
    bsz, t, _ = qkv.shape
    nt = t // TOKEN_TILE
    fwd = lambda b, s: (b, s, 0)
    bwd = lambda b, s: (b, _bwd_tile(s, nct, nt), 0)
    fwd_g = lambda b, s: (b, s, gate_blk)
    bwd_g = lambda b, s: (b, _bwd_tile(s, nct, nt), gate_blk)
    fwd_t = lambda b, s: (b, 0, s)
    bwd_t = lambda b, s: (b, 0, _bwd_tile(s, nct, nt))
    out = jax.ShapeDtypeStruct((bsz, t, out_w), F32)
    return pl.pallas_call(
        kernel_fn,
        grid=(bsz, nt),
        in_specs=[
            pl.BlockSpec((None, TOKEN_TILE, qkv_w), fwd),
            pl.BlockSpec((None, TOKEN_TILE, LANES), fwd_g),
            pl.BlockSpec((None, N_GATES, TOKEN_TILE), fwd_t),
            pl.BlockSpec((None, TOKEN_TILE, qkv_w), bwd),
            pl.BlockSpec((None, TOKEN_TILE, LANES), bwd_g),
            pl.BlockSpec((None, N_GATES, TOKEN_TILE), bwd_t),
        ] + [pl.BlockSpec(a.shape, lambda b, s: (0, 0)) for a in small],
        out_specs=[pl.BlockSpec((None, TOKEN_TILE, out_w), fwd),
                   pl.BlockSpec((None, TOKEN_TILE, out_w), bwd)],
        out_shape=[out, out],
        scratch_shapes=scratch,
        compiler_params=_params(("parallel", "arbitrary")),
        name=name,
    )(qkv, gate_src, gates_t, qkv, gate_src, gates_t, *small)


def _mlstm_kernel(qkv_f, g_f, gt_f, qkv_b, g_b, gt_b, fbr_ref, fbc_ref, hf_ref, hb_ref,
                  c_scr, n_scr, m_scr):
    @pl.when(pl.program_id(1) == 0)
    def _():
        c_scr[...] = jnp.zeros_like(c_scr)
        n_scr[...] = jnp.zeros_like(n_scr)
        m_scr[...] = jnp.zeros_like(m_scr)

    n = TOKEN_TILE
    row = lax.broadcasted_iota(jnp.int32, (n, n), 0)
    col = lax.broadcasted_iota(jnp.int32, (n, n), 1)
    lower, upper = col <= row, col >= row
    lower_s, upper_s = lower.astype(BF16), upper.astype(BF16)
    fbr = fbr_ref[...]
    fbc = fbc_ref[...]
    dirs = ((qkv_f, g_f, gt_f, hf_ref, lower, lower_s, upper_s),
            (qkv_b, g_b, gt_b, hb_ref, upper, upper_s, lower_s))

    probs = []
    for di, (qkv, g_ref, gt_ref, h_ref, incl, csum, csum_t) in enumerate(dirs):
        g = g_ref[:, 0:N_GATES]
        gt = gt_ref[...]
        ls = _log_sigmoid(g + fbr)
        lst = _log_sigmoid(gt + fbc)
        bc_all = _sel_dot(csum, ls)
        br_all = _dot_sel(lst, csum_t)
        tot = jnp.sum(ls, axis=0, keepdims=True)
        for h in range(M_HEADS):
            ic = di * 8 + h
            fc = di * 8 + 4 + h
            base = h * 2 * M_QK
            st = di * M_HEADS + h
            probs.append(dict(
                st=st, h=h, h_ref=h_ref, incl=incl,
                q=qkv[:, base:base + M_QK],
                k=qkv[:, base + M_QK:base + 2 * M_QK] * (M_QK ** -0.5),
                v=qkv[:, 2 * M_QKW + h * M_V:2 * M_QKW + (h + 1) * M_V],
                i_c=g[:, ic:ic + 1], b_c=bc_all[:, fc:fc + 1], i_r=gt[ic:ic + 1, :],
                b_r=br_all[fc:fc + 1, :], btot=tot[:, fc:fc + 1],
                c=c_scr[st], n=n_scr[st], m=m_scr[st]))
    for p in probs:
        p["qk"] = _dot_nt(p["q"], p["k"])
        p["qc"] = _dot(p["q"], p["c"])
    for p in probs:
        d = jnp.where(p["incl"], p["b_c"] - p["b_r"] + p["i_r"], -jnp.inf)
        inter = p["b_c"] + p["m"]
        p["m_row"] = jnp.maximum(inter, jnp.max(d, axis=-1, keepdims=True))
        p["e_inter"] = jnp.exp(inter - p["m_row"])
        p["s"] = p["qk"] * jnp.exp(d - p["m_row"])
    for p in probs:
        a_r = p["btot"] - p["b_r"] + p["i_r"]
        a_c = p["btot"] - p["b_c"] + p["i_c"]
        m_new = jnp.maximum(p["btot"] + p["m"], jnp.max(a_r, axis=-1, keepdims=True))
        p["carry_w"] = jnp.exp(p["btot"] + p["m"] - m_new)
        p["kw"] = p["k"] * jnp.exp(a_c - m_new)
        m_scr[p["st"]] = m_new
    for p in probs:
        num = _dot(p["s"], p["v"]) + p["e_inter"] * p["qc"]
        den = (jnp.sum(p["s"], axis=-1, keepdims=True)
               + p["e_inter"] * jnp.sum(p["q"] * p["n"], axis=-1, keepdims=True))
        hh = num / jnp.maximum(jnp.abs(den), jnp.exp(-p["m_row"]))
        p["h_ref"][:, p["h"] * M_V:(p["h"] + 1) * M_V] = hh
    for p in probs:
        c_scr[p["st"]] = p["carry_w"] * p["c"] + _dot_tn(p["kw"], p["v"])
        n_scr[p["st"]] = p["carry_w"] * p["n"] + jnp.sum(p["kw"], axis=0, keepdims=True)


def _mlstm(pm, gates_t, fb_row, fb_col, *, nct):
    scratch = [pltpu.VMEM((2 * M_HEADS, M_QK, M_V), F32),
               pltpu.VMEM((2 * M_HEADS, 1, M_QK), F32),
               pltpu.VMEM((2 * M_HEADS, 1, 1), F32)]
    return _recurrent_call(_mlstm_kernel, "mlstm", pm, 2 * M_QKW + M_VW, pm, PM_GATE_BLK, gates_t,
                           [fb_row, fb_col], M_VW, scratch, nct=nct)


def _unit_tri_inverse_many(a_list, eye):
    n_iter = int(math.log2(CHUNK)) - 1
    xs = [(eye - a).astype(BF16) for a in a_list]
    ps = [a.astype(BF16) for a in a_list]
    for it in range(n_iter):
        ps = [jnp.dot(p, p, preferred_element_type=F32).astype(BF16) for p in ps]
        xs = [x.astype(F32) + jnp.dot(x, p, preferred_element_type=F32) for x, p in zip(xs, ps)]
        if it < n_iter - 1:
            xs = [x.astype(BF16) for x in xs]
    out = []
    for a, x in zip(a_list, xs):
        x2 = _split2(x)
        r = (eye - x) - _dot_f32ish(_split2(a), x2)
        out.append(x + jnp.dot(x2[0], r.astype(BF16), preferred_element_type=F32))
    return out


def _gdn_kernel(qkv_f, g_f, gt_f, qkv_b, g_b, gt_b, par_r, par_c, of_ref, ob_ref, s_scr):
    @pl.when(pl.program_id(1) == 0)
    def _():
        s_scr[...] = jnp.zeros_like(s_scr)

    n = TOKEN_TILE
    nch = n // CHUNK
    row = lax.broadcasted_iota(jnp.int32, (n, n), 0)
    col = lax.broadcasted_iota(jnp.int32, (n, n), 1)
    same = (row // CHUNK) == (col // CHUNK)
    lower, upper = same & (col <= row), same & (col >= row)
    lower_x, upper_x = same & (col < row), same & (col > row)
    lower_s, upper_s, same_s = lower.astype(BF16), upper.astype(BF16), same.astype(BF16)
    eye = jnp.where(row == col, 1.0, 0.0)
    alog_r, dtb_r = par_r[0:1, :], par_r[1:2, :]
    alog_c, dtb_c = par_c[:, 0:1], par_c[:, 1:2]
    col_r = lax.broadcasted_iota(jnp.int32, (n, N_GATES), 1)
    row_c = lax.broadcasted_iota(jnp.int32, (N_GATES, n), 0)
    dirs = ((qkv_f, g_f, gt_f, of_ref, lower, lower_x, lower_s, upper_s),
            (qkv_b, g_b, gt_b, ob_ref, upper, upper_x, upper_s, lower_s))

    probs = []
    for di, (qkv, g_ref, gt_ref, o_ref, incl, strict, csum, csum_t) in enumerate(dirs):
        g = g_ref[:, 0:N_GATES]
        gt = gt_ref[...]
        val = jnp.where(col_r < 8, -jnp.exp(alog_r) * _softplus(g + dtb_r), _sigmoid(g))
        val_t = jnp.where(row_c < 8, -jnp.exp(alog_c) * _softplus(gt + dtb_c), _sigmoid(gt))
        gc_all = _sel_dot(csum, val)
        gr_all = _dot_sel(val_t, csum_t)
        ge_all = _sel_dot(same_s, val)
        for h in range(G_HEADS):
            ac = di * 4 + h
            bc = 8 + di * 4 + h
            q = qkv[:, h * G_HD:(h + 1) * G_HD]
            k = qkv[:, G_W + h * G_HD:G_W + (h + 1) * G_HD]
            v = qkv[:, 2 * G_W + h * G_HD:2 * G_W + (h + 1) * G_HD]
            gc_c, gc_r = gc_all[:, ac:ac + 1], gr_all[ac:ac + 1, :]
            beta_c, ge_c = val[:, bc:bc + 1], ge_all[:, ac:ac + 1]
            decay = jnp.where(incl, jnp.exp(jnp.where(incl, gc_c - gc_r, 0.0)), 0.0)
            kb = k * beta_c
            eg = jnp.exp(gc_c)
            probs.append(dict(
                di=di, h=h, o_ref=o_ref,
                a=jnp.where(strict, _dot_nt(kb, k) * decay, 0.0),
                rhs=jnp.concatenate([v * beta_c, kb * eg], axis=-1),
                k_end=k * jnp.exp(ge_c - gc_c),
                qk=jnp.where(incl, _dot_nt(q, k) * decay, 0.0),
                qg=q * eg,
                carry=jnp.exp(ge_c),
                s=s_scr[di * G_HEADS + h]))
    t_inv = _unit_tri_inverse_many([p["a"] for p in probs], eye)
    for p, t in zip(probs, t_inv):
        uw = _dot(t, p["rhs"])
        p["u"], p["w"] = uw[:, 0:G_HD], uw[:, G_HD:]
        p["v_new"] = [None] * nch
        p["o_inter"] = [None] * nch

    for ci in range(nch):
        rows = [slice(cc * CHUNK, (cc + 1) * CHUNK)
                for cc in [ci if p["di"] == 0 else nch - 1 - ci for p in probs]]
        s_bf = [p["s"].astype(BF16) for p in probs]
        v_new = [p["u"][r] - jnp.dot(p["w"][r].astype(BF16), sb, preferred_element_type=F32)
                 for p, r, sb in zip(probs, rows, s_bf)]
        upd = [_dot_tn(p["k_end"][r], vn) for p, r, vn in zip(probs, rows, v_new)]
        for p, r, sb, vn, up in zip(probs, rows, s_bf, v_new, upd):
            cc = r.start // CHUNK
            p["o_inter"][cc] = jnp.dot(p["qg"][r].astype(BF16), sb, preferred_element_type=F32)
            p["v_new"][cc] = vn
            p["s"] = p["s"] * p["carry"][r.start:r.start + 1, :] + up
    for p in probs:
        o = jnp.concatenate(p["o_inter"], axis=0) + _dot(p["qk"], jnp.concatenate(p["v_new"], axis=0))
        s_scr[p["di"] * G_HEADS + p["h"]] = p["s"]
        p["o_ref"][:, p["h"] * G_HD:(p["h"] + 1) * G_HD] = o


def _gdn(qkv, pg, gates_t, par_row, par_col, *, nct):
    scratch = [pltpu.VMEM((2 * G_HEADS, G_HD, G_HD), F32)]
    return _recurrent_call(_gdn_kernel, "gdn", qkv, 3 * G_W, pg, PG_GATE_BLK, gates_t,
                           [par_row, par_col], G_W, scratch, nct=nct)


def _post_kernel(x_ref, mod_ref, gain_ref, ya_ref, hf_ref, hb_ref, o_ref, of_ref, ob_ref, z_ref,
                 gs_ref, mn_ref, gn_ref, wb_ref, wo_ref, w1_ref, w2_ref, out_ref):
    hs = hf_ref[...] + hb_ref[...]
    og = o_ref[...]
    os_ = of_ref[...] + ob_ref[...]
    z = z_ref[...]
    ym, yg = [], []
    for h in range(M_HEADS):
        sl = slice(h * M_V, (h + 1) * M_V)
        ym.append(_rms(hs[:, sl], mn_ref[:, sl]) * _sigmoid(og[:, sl]))
    for h in range(G_HEADS):
        sl = slice(h * G_HD, (h + 1) * G_HD)
        zz = z[:, sl]
        yg.append(_rms(os_[:, sl], gn_ref[...]) * (zz * _sigmoid(zz)))
    ym = jnp.concatenate(ym, axis=-1)
    yg = jnp.concatenate(yg, axis=-1)
    y = (gs_ref[:, 0:D_MODEL].astype(F32) * _dot(ya_ref[...], wb_ref[0])
         + gs_ref[:, D_MODEL:2 * D_MODEL].astype(F32) * _dot(ym, wb_ref[1])
         + gs_ref[:, 2 * D_MODEL:].astype(F32) * _dot(yg, wb_ref[2]))
    y2 = _dot(y, wo_ref[...])
    x = x_ref[...] + mod_ref[2:3, :] * _rms(y2, gain_ref[1:2, :])

    h = (_rms(x, gain_ref[2:3, :]) * (1.0 + mod_ref[4:5, :]) + mod_ref[3:4, :]).astype(BF16)
    acc = jnp.zeros((x.shape[0], D_MODEL), F32)
    step = 1024
    for j in range(D_FF // step):
        a = jnp.maximum(jnp.dot(h, w1_ref[:, j * step:(j + 1) * step], preferred_element_type=F32), 0.0)
        acc = acc + jnp.dot((a * a).astype(BF16), w2_ref[j * step:(j + 1) * step, :], preferred_element_type=F32)
    out_ref[...] = x + mod_ref[5:6, :] * _rms(acc, gain_ref[3:4, :])


def _post(x_all, mod, gains, ya, hf, hb, pm, of, ob, pg, gsig, m_norm, g_norm, wb, wo, w1, w2, *, nct, t_off):
    bsz, t, _ = x_all.shape
    nt = t // TOKEN_TILE - t_off
    tile = lambda w, blk=0: pl.BlockSpec((None, TOKEN_TILE, w), lambda b, i: (b, i + t_off, blk))
    const = lambda shape: pl.BlockSpec(shape, lambda b, i: (0,) * len(shape), pipeline_mode=pl.Buffered(1))
    return pl.pallas_call(
        _post_kernel,
        grid=(bsz, nt),
        in_specs=[
            tile(D_MODEL),
            pl.BlockSpec((None, 6, D_MODEL), lambda b, i: (2 * b + (i + t_off >= nct).astype(jnp.int32), 0, 0)),
            const((4, D_MODEL)),
            tile(A_VW), tile(M_VW), tile(M_VW),
            tile(M_VW, (2 * M_QKW + M_VW) // M_VW),
            tile(G_W), tile(G_W),
            tile(G_W, 0),
            tile(3 * D_MODEL),
            const((1, M_VW)), const((1, G_HD)),
            const((3, BRANCH_W, D_MODEL)), const((D_MODEL, D_MODEL)),
            const((D_MODEL, D_FF)), const((D_FF, D_MODEL)),
        ],
        out_specs=pl.BlockSpec((None, TOKEN_TILE, D_MODEL), lambda b, i: (b, i, 0)),
        out_shape=jax.ShapeDtypeStruct((bsz, nt * TOKEN_TILE, D_MODEL), F32),
        compiler_params=_params(("parallel", "parallel")),
        name="post",
    )(x_all, mod, gains, ya, hf, hb, pm, of, ob, pg, gsig, m_norm, g_norm, wb, wo, w1, w2)


def _rope_tables(n_lat):
    rows = n_lat // GRID_W
    row = jnp.repeat(jnp.arange(rows, dtype=F32), GRID_W)
    col = jnp.tile(jnp.arange(GRID_W, dtype=F32), rows)
    half = A_HD // 2
    inv_freq = ROPE_BASE ** (-jnp.arange(0, half, 2, dtype=F32) / half)
    ang_r = row[:, None] * inv_freq
    ang_c = col[:, None] * inv_freq
    ang = jnp.concatenate([ang_r, ang_r, ang_c, ang_c] * 2, axis=-1)
    return jnp.cos(ang), jnp.sin(ang)


def _pad_cols(a, width):
    return jnp.pad(a, [(0, 0)] * (a.ndim - 1) + [(0, width - a.shape[-1])])


def _split_in_proj(w, b):
    def both(f):
        return f(w).astype(BF16), f(b[None, :])

    qk_idx = np.concatenate([np.concatenate([np.arange(_O_MQ + h * M_QK, _O_MQ + (h + 1) * M_QK),
                                             np.arange(_O_MK + h * M_QK, _O_MK + (h + 1) * M_QK)])
                             for h in range(M_HEADS)])
    attn = both(lambda a: a[:, _O_AQ:_O_MQ])
    mlstm = both(lambda a: jnp.concatenate(
        [a[:, qk_idx], a[:, _O_MV:_O_MG], _pad_cols(a[:, _O_MG:_O_GQ], LANES)], axis=-1))
    gdn_qkv = both(lambda a: a[:, _O_GQ:_O_GZ])
    gdn = both(lambda a: jnp.concatenate(
        [a[:, _O_GZ:_O_GG], _pad_cols(a[:, _O_GG:_O_MRG], LANES)], axis=-1))
    mrg = both(lambda a: a[:, _O_MRG:_O_END])
    return attn, mlstm, gdn_qkv, gdn, mrg


def kernel(x, c, ctx, c_ctx, w_ada, b_ada, norm_gains, w_in, b_in, a_lambda, a_norm, m_fbias, m_norm,
           g_conv, g_alog, g_dtbias, g_norm, w_branch, w_out, w_ff1, w_ff2):
    bsz, n_lat, _ = x.shape
    tc = ctx.shape[1]
    depth = w_ada.shape[0]
    assert tc % TOKEN_TILE == 0 and n_lat % TOKEN_TILE == 0 and TOKEN_TILE % CHUNK == 0
    nct = tc // TOKEN_TILE

    rows = -(-(bsz + 1) // 8) * 8
    cond = jnp.zeros((rows, D_MODEL), F32).at[:bsz].set(c).at[bsz].set(c_ctx)
    mod_all = _ada_all(cond, w_ada, b_ada)
    ctx_rows = jnp.broadcast_to(mod_all[:, bsz:bsz + 1], (depth, bsz, 6 * D_MODEL))
    mod_all = jnp.stack([ctx_rows, mod_all[:, :bsz]], axis=2).reshape(depth, 2 * bsz, 6, D_MODEL)

    rope = _rope_tables(n_lat)
    x_all = jnp.concatenate([ctx, x], axis=1)
    zeros4 = jnp.zeros((4,), F32)

    for i in range(depth):
        lam_init = 0.8 - 0.6 * math.exp(-0.3 * i)
        last = i == depth - 1
        mod = mod_all[i]
        gains = norm_gains[i]
        (wa, ba), (wm, bm), (wq, bq), (wg, bg), (wr, br) = _split_in_proj(w_in[i], b_in[i])
        g0 = gains[0:1]

        pa, pm, qkv, pg, gsig = _proj(x_all, mod, g0, (wa, wm, wq, wg, wr), (ba, bm, bq, bg, br),
                                      g_conv[i], rope, nct=nct)

        ya = _attention(pa, a_lambda[i], a_norm[i], nct=nct, tc=tc, lam_init=lam_init)

        fb = jnp.concatenate([zeros4, m_fbias[i, 0], zeros4, m_fbias[i, 1]])
        m_gt = jnp.swapaxes(pm[:, :, PM_GATE_BLK * LANES:PM_GATE_BLK * LANES + N_GATES], 1, 2)
        hf, hb = _mlstm(pm, m_gt, fb[None, :], fb[:, None], nct=nct)

        alog = jnp.concatenate([g_alog[i].reshape(-1), jnp.zeros((8,), F32)])
        dtb = jnp.concatenate([g_dtbias[i].reshape(-1), jnp.zeros((8,), F32)])
        par = jnp.stack([alog, dtb])
        g_gt = jnp.swapaxes(pg[:, :, PG_GATE_BLK * LANES:PG_GATE_BLK * LANES + N_GATES], 1, 2)
        of, ob = _gdn(qkv, pg, g_gt, par, par.T, nct=nct)

        t_off = nct if last else 0
        x_all = _post(x_all, mod, gains, ya, hf, hb, pm, of, ob, pg, gsig,
                      m_norm[i].reshape(1, M_VW), g_norm[i].reshape(1, G_HD),
                      w_branch[i].astype(BF16), w_out[i].astype(BF16),
                      w_ff1[i].astype(BF16), w_ff2[i].astype(BF16), nct=nct, t_off=t_off)
    return x_all
```

```python
import functools
import math

import jax
import jax.numpy as jnp
import numpy as np
from jax import lax
from jax.experimental import pallas as pl
from jax.experimental.pallas import tpu as pltpu

F32 = jnp.float32
BF16 = jnp.bfloat16

D_MODEL = 1024
GRID_W = 64
A_HEADS = 4
A_HD = 64
A_VD = 2 * A_HD
A_QK = A_HEADS * 2 * A_HD
A_VW = A_HEADS * A_VD
M_HEADS = 4
M_QK = 64
M_V = 128
M_QKW = M_HEADS * M_QK
M_VW = M_HEADS * M_V
G_HEADS = 4
G_HD = 128
G_W = G_HEADS * G_HD
CONV_K = 5
BRANCH_W = 512
D_FF = 4 * D_MODEL
ROPE_BASE = 10000.0
EPS = 1e-6
LOG2_E = math.log2(math.e)
N_GATES = 16

LANES = 128
TOKEN_TILE = 256
CHUNK = 64
VMEM_LIMIT = 56 * 1024 * 1024

_OFF = np.cumsum([0, A_QK, A_QK, A_VW, M_QKW, M_QKW, M_VW, M_VW, 4 * M_HEADS,
                  G_W, G_W, G_W, G_W, 4 * G_HEADS, 3 * D_MODEL])
(_O_AQ, _O_AK, _O_AV, _O_MQ, _O_MK, _O_MV, _O_MO, _O_MG,
 _O_GQ, _O_GK, _O_GV, _O_GZ, _O_GG, _O_MRG, _O_END) = [int(v) for v in _OFF]

PA_W = 3 * A_QK
PM_W = M_QKW * 2 + 2 * M_VW + LANES
PM_GATE_BLK = (M_QKW * 2 + 2 * M_VW) // LANES
PG_W = G_W + LANES
PG_GATE_BLK = G_W // LANES
HALO = 8


def _params(sem):
    return pltpu.CompilerParams(dimension_semantics=sem, vmem_limit_bytes=VMEM_LIMIT)


def _sigmoid(x):
    return 1.0 / (1.0 + jnp.exp(-x))


def _softplus(x):
    return jnp.maximum(x, 0.0) + jnp.log1p(jnp.exp(-jnp.abs(x)))


def _log_sigmoid(x):
    return -_softplus(-x)


def _rms(x, gain):
    return x * lax.rsqrt(jnp.mean(x * x, axis=-1, keepdims=True) + EPS) * gain


def _dot(a, b):
    return jnp.dot(a.astype(BF16), b.astype(BF16), preferred_element_type=F32)


def _dot_nt(a, b):
    return lax.dot_general(a.astype(BF16), b.astype(BF16), (((1,), (1,)), ((), ())),
                           preferred_element_type=F32)


def _dot_tn(a, b):
    return lax.dot_general(a.astype(BF16), b.astype(BF16), (((0,), (0,)), ((), ())),
                           preferred_element_type=F32)


def _dot_exact(a, b):
    return jnp.dot(a, b, precision=lax.Precision.HIGHEST, preferred_element_type=F32)


def _ada_kernel(c_ref, w_ref, b_ref, o_ref):
    c = c_ref[...]
    s = c * _sigmoid(c)
    o_ref[...] = _dot_exact(s, w_ref[...]) + b_ref[...]


def _ada_all(cond, w_ada, b_ada):
    depth = w_ada.shape[0]
    rows = cond.shape[0]
    n = w_ada.shape[2]
    tn = 1024
    return pl.pallas_call(
        _ada_kernel,
        grid=(depth, n // tn),
        in_specs=[pl.BlockSpec((rows, D_MODEL), lambda l, j: (0, 0)),
                  pl.BlockSpec((None, D_MODEL, tn), lambda l, j: (l, 0, j)),
                  pl.BlockSpec((None, 1, tn), lambda l, j: (l, 0, j))],
        out_specs=pl.BlockSpec((None, rows, tn), lambda l, j: (l, 0, j)),
        out_shape=jax.ShapeDtypeStruct((depth, rows, n), F32),
        compiler_params=_params(("parallel", "parallel")),
        name="ada",
    )(cond, w_ada, b_ada.reshape(depth, 1, n))


def _rot_half(x):
    lane = lax.broadcasted_iota(jnp.int32, x.shape, 1)
    first = (lane % 32) < 16
    return jnp.where(first, -pltpu.roll(x, LANES - 16, 1), pltpu.roll(x, 16, 1))


def _proj_kernel(x_ref, xp_ref, xn_ref, mod_ref, g_ref, wa_ref, wm_ref, wq_ref, wg_ref, wr_ref,
                 ba_ref, bm_ref, bq_ref, bg_ref, br_ref, cw_ref, cos_ref, sin_ref,
                 pa_ref, pm_ref, qkv_ref, pg_ref, gs_ref, *, nct, nt):
    tile = pl.program_id(1)
    x = x_ref[...]
    xe = jnp.concatenate([xp_ref[...], x, xn_ref[...]], axis=0)
    he = (_rms(xe, g_ref[...]) * (1.0 + mod_ref[1:2, :]) + mod_ref[0:1, :]).astype(BF16)
    h = he[HALO:HALO + TOKEN_TILE]

    def project(w_ref, b_ref, lhs=h):
        return jnp.dot(lhs, w_ref[...], preferred_element_type=F32) + b_ref[...]

    ext = project(wq_ref, bq_ref, he)
    n_ext = ext.shape[0]
    first = (tile == 0) | (tile == nct)
    last = (tile == nct - 1) | (tile == nt - 1)
    row = lax.broadcasted_iota(jnp.int32, (TOKEN_TILE, 1), 0)
    half = CONV_K // 2
    cw = cw_ref[...]
    conv = ext[HALO:HALO + TOKEN_TILE] * cw[half:half + 1, :]
    for d in range(-half, half + 1):
        if d == 0:
            continue
        shifted = pltpu.roll(ext, (-d) % n_ext, 0)[HALO:HALO + TOKEN_TILE]
        bad = ((row + d < 0) & first) | ((row + d >= TOKEN_TILE) & last)
        conv = conv + jnp.where(bad, 0.0, shifted) * cw[half + d:half + d + 1, :]
    y = conv * _sigmoid(conv)
    for j in range(3 * G_HEADS):
        sl = slice(j * G_HD, (j + 1) * G_HD)
        yh = y[:, sl]
        if j < 2 * G_HEADS:
            yh = yh * lax.rsqrt(jnp.sum(yh * yh, axis=-1, keepdims=True) + EPS)
        if j < G_HEADS:
            yh = yh * (G_HD ** -0.5)
        qkv_ref[:, sl] = yh

    acc = project(wa_ref, ba_ref)
    is_lat = tile >= nct
    cos = jnp.where(is_lat, cos_ref[...], 1.0)
    sin = jnp.where(is_lat, sin_ref[...], 0.0)
    for j in range(2 * A_QK // LANES):
        blk = acc[:, j * LANES:(j + 1) * LANES]
        r = blk * cos + _rot_half(blk) * sin
        if j < A_QK // LANES:
            r = r * (A_HD ** -0.5 * LOG2_E)
        pa_ref[:, j * LANES:(j + 1) * LANES] = r.astype(pa_ref.dtype)
    pa_ref[:, 2 * A_QK:] = acc[:, 2 * A_QK:].astype(pa_ref.dtype)
    pm_ref[...] = project(wm_ref, bm_ref)
    pg_ref[...] = project(wg_ref, bg_ref)
    gs_ref[...] = _sigmoid(project(wr_ref, br_ref)).astype(gs_ref.dtype)


def _proj(x_all, mod, gain, weights, biases, conv_w, rope, *, nct):
    bsz, t, _ = x_all.shape
    nt = t // TOKEN_TILE
    per_tile = TOKEN_TILE // HALO
    widths = [w.shape[1] for w in weights]
    const = lambda shape: pl.BlockSpec(shape, lambda i, j: (0, 0), pipeline_mode=pl.Buffered(1))
    rope_spec = pl.BlockSpec((TOKEN_TILE, LANES), lambda i, j: (jnp.maximum(j - nct, 0), 0))
    tile = lambda n: pl.BlockSpec((None, TOKEN_TILE, n), lambda i, j: (i, j, 0))
    prev_rows = pl.BlockSpec((None, HALO, D_MODEL), lambda i, j: (i, jnp.maximum(j * per_tile - 1, 0), 0))
    next_rows = pl.BlockSpec((None, HALO, D_MODEL),
                             lambda i, j: (i, jnp.minimum((j + 1) * per_tile, nt * per_tile - 1), 0))
    return pl.pallas_call(
        functools.partial(_proj_kernel, nct=nct, nt=nt),
        grid=(bsz, nt),
        in_specs=[tile(D_MODEL), prev_rows, next_rows,
                  pl.BlockSpec((None, 6, D_MODEL), lambda i, j: (2 * i + (j >= nct).astype(jnp.int32), 0, 0)),
                  const((1, D_MODEL))]
                 + [const((D_MODEL, n)) for n in widths] + [const((1, n)) for n in widths]
                 + [const(conv_w.shape), rope_spec, rope_spec],
        out_specs=[tile(n) for n in widths],
        out_shape=[jax.ShapeDtypeStruct((bsz, t, n), dt) for n, dt in zip(widths, (BF16, F32, F32, F32, BF16))],
        compiler_params=_params(("parallel", "parallel")),
        name="proj",
    )(x_all, x_all, x_all, mod, gain, *weights, *biases, conv_w, *rope)


def _attn_kernel(q_ref, k_ref, v_ref, lam_ref, an_ref, o_ref, *, nct, tc, t_all, lam_init):
    qi = pl.program_id(2)
    q = q_ref[...]
    lane = lax.broadcasted_iota(jnp.int32, q.shape, 1)
    zero = jnp.zeros_like(q)
    qm = (jnp.where(lane < A_HD, q, zero), jnp.where(lane >= A_HD, q, zero))
    lv = lam_ref[...].astype(F32)
    lam = (jnp.exp(jnp.sum(lv[0:1] * lv[1:2], axis=-1, keepdims=True))
           - jnp.exp(jnp.sum(lv[2:3] * lv[3:4], axis=-1, keepdims=True)) + lam_init)

    def attend(n_keys):
        pv = []
        for mp in range(2):
            s = lax.dot_general(qm[mp], k_ref[0:n_keys, :], (((1,), (1,)), ((), ())),
                                preferred_element_type=F32)
            p = jnp.exp2(s - jnp.max(s, axis=-1, keepdims=True))
            l = jnp.sum(p, axis=-1, keepdims=True)
            pv.append(jnp.dot(p.astype(BF16), v_ref[0:n_keys, :], preferred_element_type=F32) / l)
        o = pv[0] - lam * pv[1]
        o_ref[...] = _rms(o, an_ref[...]) * (1.0 - lam_init)

    @pl.when(qi < nct)
    def _():
        attend(tc)

    @pl.when(qi >= nct)
    def _():
        attend(t_all)


def _attention(pa, a_lambda, a_norm, *, nct, tc, lam_init):
    bsz, t, _ = pa.shape
    nt = t // TOKEN_TILE
    kb = A_QK // LANES
    return pl.pallas_call(
        functools.partial(_attn_kernel, nct=nct, tc=tc, t_all=t, lam_init=lam_init),
        grid=(bsz, A_HEADS, nt),
        in_specs=[
            pl.BlockSpec((None, TOKEN_TILE, LANES), lambda b, h, i: (b, i, h)),
            pl.BlockSpec((None, t, LANES), lambda b, h, i: (b, 0, kb + h)),
            pl.BlockSpec((None, t, LANES), lambda b, h, i: (b, 0, 2 * kb + h)),
            pl.BlockSpec((4, A_HD), lambda b, h, i: (0, 0)),
            pl.BlockSpec((1, A_VD), lambda b, h, i: (0, 0)),
        ],
        out_specs=pl.BlockSpec((None, TOKEN_TILE, LANES), lambda b, h, i: (b, i, h)),
        out_shape=jax.ShapeDtypeStruct((bsz, t, A_VW), F32),
        compiler_params=_params(("parallel", "parallel", "arbitrary")),
        name="diff_attn",
    )(pa, pa, pa, a_lambda, a_norm.reshape(1, A_VD))


def _bwd_tile(s, nct, nt):
    return jnp.where(s < nct, nct - 1 - s, nt - 1 - (s - nct))


def _split2(x):
    hi = x.astype(BF16)
    return hi, (x - hi.astype(F32)).astype(BF16)


def _split3(x):
    hi = x.astype(BF16)
    r = x - hi.astype(F32)
    mid = r.astype(BF16)
    return hi, mid, (r - mid.astype(F32)).astype(BF16)


def _dot_f32ish(a2, b2):
    (ah, al), (bh, bl) = a2, b2
    return (jnp.dot(ah, bl, preferred_element_type=F32) + jnp.dot(al, bh, preferred_element_type=F32)
            + jnp.dot(ah, bh, preferred_element_type=F32))


def _sel_dot(sel, x):
    hi, mid, lo = _split3(x)
    return (jnp.dot(sel, lo, preferred_element_type=F32) + jnp.dot(sel, mid, preferred_element_type=F32)
            + jnp.dot(sel, hi, preferred_element_type=F32))


def _dot_sel(x, sel):
    hi, mid, lo = _split3(x)
    return (jnp.dot(lo, sel, preferred_element_type=F32) + jnp.dot(mid, sel, preferred_element_type=F32)
            + jnp.dot(hi, sel, preferred_element_type=F32))


def _recurrent_call(kernel_fn, name, qkv, qkv_w, gate_src, gate_blk, gates_t, small, out_w, scratch, *, nct):
    bsz, t, _ = qkv.shape
    nt = t // TOKEN_TILE
    fwd = lambda b, s: (b, s, 0)
    bwd = lambda b, s: (b, _bwd_tile(s, nct, nt), 0)
    fwd_g = lambda b, s: (b, s, gate_blk)
    bwd_g = lambda b, s: (b, _bwd_tile(s, nct, nt), gate_blk)
    fwd_t = lambda b, s: (b, 0, s)
    bwd_t = lambda b, s: (b, 0, _bwd_tile(s, nct, nt))
    out = jax.ShapeDtypeStruct((bsz, t, out_w), F32)
    return pl.pallas_call(
        kernel_fn,
        grid=(bsz, nt),
        in_specs=[
            pl.BlockSpec((None, TOKEN_TILE, qkv_w), fwd),
            pl.BlockSpec((None, TOKEN_TILE, LANES), fwd_g),
            pl.BlockSpec((None, N_GATES, TOKEN_TILE), fwd_t),
            pl.BlockSpec((None, TOKEN_TILE, qkv_w), bwd),
            pl.BlockSpec((None, TOKEN_TILE, LANES), bwd_g),
            pl.BlockSpec((None, N_GATES, TOKEN_TILE), bwd_t),
        ] + [pl.BlockSpec(a.shape, lambda b, s: (0, 0)) for a in small],
        out_specs=[pl.BlockSpec((None, TOKEN_TILE, out_w), fwd),
                   pl.BlockSpec((None, TOKEN_TILE, out_w), bwd)],
        out_shape=[out, out],
        scratch_shapes=scratch,
        compiler_params=_params(("parallel", "arbitrary")),
        name=name,
    )(qkv, gate_src, gates_t, qkv, gate_src, gates_t, *small)


def _mlstm_kernel(qkv_f, g_f, gt_f, qkv_b, g_b, gt_b, fbr_ref, fbc_ref, hf_ref, hb_ref,
                  c_scr, m_scr):
    @pl.when(pl.program_id(1) == 0)
    def _():
        c_scr[...] = jnp.zeros_like(c_scr)
        m_scr[...] = jnp.zeros_like(m_scr)

    n = TOKEN_TILE
    row = lax.broadcasted_iota(jnp.int32, (n, n), 0)
    col = lax.broadcasted_iota(jnp.int32, (n, n), 1)
    lower, upper = col <= row, col >= row
    lower_s, upper_s = lower.astype(BF16), upper.astype(BF16)
    fbr = fbr_ref[...]
    fbc = fbc_ref[...]
    dirs = ((qkv_f, g_f, gt_f, hf_ref, lower, lower_s, upper_s),
            (qkv_b, g_b, gt_b, hb_ref, upper, upper_s, lower_s))

    probs = []
    for di, (qkv, g_ref, gt_ref, h_ref, incl, csum, csum_t) in enumerate(dirs):
        g = g_ref[:, 0:N_GATES]
        gt = gt_ref[...]
        ls = _log_sigmoid(g + fbr)
        lst = _log_sigmoid(gt + fbc)
        bc_all = _sel_dot(csum, ls)
        br_all = _dot_sel(lst, csum_t)
        tot = jnp.sum(ls, axis=0, keepdims=True)
        for h in range(M_HEADS):
            ic = di * 8 + h
            fc = di * 8 + 4 + h
            base = h * 2 * M_QK
            st = di * M_HEADS + h
            probs.append(dict(
                st=st, h=h, h_ref=h_ref, incl=incl,
                q=qkv[:, base:base + M_QK],
                k=qkv[:, base + M_QK:base + 2 * M_QK] * (M_QK ** -0.5),
                v=qkv[:, 2 * M_QKW + h * M_V:2 * M_QKW + (h + 1) * M_V],
                i_c=g[:, ic:ic + 1], b_c=bc_all[:, fc:fc + 1], i_r=gt[ic:ic + 1, :],
                b_r=br_all[fc:fc + 1, :], btot=tot[:, fc:fc + 1],
                c=c_scr[st], m=m_scr[st]))
    ones = jnp.ones((n, M_V), BF16)
    for p in probs:
        p["v2"] = jnp.concatenate([p["v"].astype(BF16), ones], axis=-1)
        p["qk"] = _dot_nt(p["q"], p["k"])
        p["qc"] = _dot(p["q"], p["c"])
    for p in probs:
        d = jnp.where(p["incl"], p["b_c"] - p["b_r"] + p["i_r"], -jnp.inf)
        inter = p["b_c"] + p["m"]
        p["m_row"] = jnp.maximum(inter, jnp.max(d, axis=-1, keepdims=True))
        p["e_inter"] = jnp.exp(inter - p["m_row"])
        p["s"] = p["qk"] * jnp.exp(d - p["m_row"])
    for p in probs:
        a_r = p["btot"] - p["b_r"] + p["i_r"]
        a_c = p["btot"] - p["b_c"] + p["i_c"]
        m_new = jnp.maximum(p["btot"] + p["m"], jnp.max(a_r, axis=-1, keepdims=True))
        p["carry_w"] = jnp.exp(p["btot"] + p["m"] - m_new)
        p["kw"] = p["k"] * jnp.exp(a_c - m_new)
        m_scr[p["st"]] = m_new
    for p in probs:
        both = jnp.dot(p["s"].astype(BF16), p["v2"], preferred_element_type=F32) + p["e_inter"] * p["qc"]
        hh = both[:, 0:M_V] / jnp.maximum(jnp.abs(both[:, M_V:]), jnp.exp(-p["m_row"]))
        p["h_ref"][:, p["h"] * M_V:(p["h"] + 1) * M_V] = hh
    for p in probs:
        c_scr[p["st"]] = p["carry_w"] * p["c"] + _dot_tn(p["kw"], p["v2"])


def _mlstm(pm, gates_t, fb_row, fb_col, *, nct):
    scratch = [pltpu.VMEM((2 * M_HEADS, M_QK, 2 * M_V), F32),
               pltpu.VMEM((2 * M_HEADS, 1, 1), F32)]
    return _recurrent_call(_mlstm_kernel, "mlstm", pm, 2 * M_QKW + M_VW, pm, PM_GATE_BLK, gates_t,
                           [fb_row, fb_col], M_VW, scratch, nct=nct)


def _unit_tri_inverse_many(a_list, eye):
    n_iter = int(math.log2(CHUNK)) - 1
    xs = [(eye - a).astype(BF16) for a in a_list]
    ps = [a.astype(BF16) for a in a_list]
    for it in range(n_iter):
        ps = [jnp.dot(p, p, preferred_element_type=F32).astype(BF16) for p in ps]
        xs = [x.astype(F32) + jnp.dot(x, p, preferred_element_type=F32) for x, p in zip(xs, ps)]
        if it < n_iter - 1:
            xs = [x.astype(BF16) for x in xs]
    out = []
    for a, x in zip(a_list, xs):
        x2 = _split2(x)
        r = (eye - x) - _dot_f32ish(_split2(a), x2)
        out.append(x + jnp.dot(x2[0], r.astype(BF16), preferred_element_type=F32))
    return out


def _gdn_kernel(qkv_f, g_f, gt_f, qkv_b, g_b, gt_b, par_r, par_c, of_ref, ob_ref, s_scr):
    @pl.when(pl.program_id(1) == 0)
    def _():
        s_scr[...] = jnp.zeros_like(s_scr)

    n = TOKEN_TILE
    nch = n // CHUNK
    row = lax.broadcasted_iota(jnp.int32, (n, n), 0)
    col = lax.broadcasted_iota(jnp.int32, (n, n), 1)
    same = (row // CHUNK) == (col // CHUNK)
    lower, upper = same & (col <= row), same & (col >= row)
    lower_x, upper_x = same & (col < row), same & (col > row)
    lower_s, upper_s, same_s = lower.astype(BF16), upper.astype(BF16), same.astype(BF16)
    eye = jnp.where(row == col, 1.0, 0.0)
    alog_r, dtb_r = par_r[0:1, :], par_r[1:2, :]
    alog_c, dtb_c = par_c[:, 0:1], par_c[:, 1:2]
    col_r = lax.broadcasted_iota(jnp.int32, (n, N_GATES), 1)
    row_c = lax.broadcasted_iota(jnp.int32, (N_GATES, n), 0)
    dirs = ((qkv_f, g_f, gt_f, of_ref, lower, lower_x, lower_s, upper_s),
            (qkv_b, g_b, gt_b, ob_ref, upper, upper_x, upper_s, lower_s))

    probs = []
    for di, (qkv, g_ref, gt_ref, o_ref, incl, strict, csum, csum_t) in enumerate(dirs):
        g = g_ref[:, 0:N_GATES]
        gt = gt_ref[...]
        val = jnp.where(col_r < 8, -jnp.exp(alog_r) * _softplus(g + dtb_r), _sigmoid(g))
        val_t = jnp.where(row_c < 8, -jnp.exp(alog_c) * _softplus(gt + dtb_c), _sigmoid(gt))
        gc_all = _sel_dot(csum, val)
        gr_all = _dot_sel(val_t, csum_t)
        ge_all = _sel_dot(same_s, val)
        for h in range(G_HEADS):
            ac = di * 4 + h
            bc = 8 + di * 4 + h
            q = qkv[:, h * G_HD:(h + 1) * G_HD]
            k = qkv[:, G_W + h * G_HD:G_W + (h + 1) * G_HD]
            v = qkv[:, 2 * G_W + h * G_HD:2 * G_W + (h + 1) * G_HD]
            gc_c, gc_r = gc_all[:, ac:ac + 1], gr_all[ac:ac + 1, :]
            beta_c, ge_c = val[:, bc:bc + 1], ge_all[:, ac:ac + 1]
            decay = jnp.where(incl, jnp.exp(jnp.where(incl, gc_c - gc_r, 0.0)), 0.0)
            kb = k * beta_c
            eg = jnp.exp(gc_c)
            probs.append(dict(
                di=di, h=h, o_ref=o_ref,
                a=jnp.where(strict, _dot_nt(kb, k) * decay, 0.0),
                rhs=jnp.concatenate([v * beta_c, kb * eg], axis=-1),
                k_end=k * jnp.exp(ge_c - gc_c),
                qk=jnp.where(incl, _dot_nt(q, k) * decay, 0.0),
                qg=q * eg,
                carry=jnp.exp(ge_c),
                s=s_scr[di * G_HEADS + h]))
    t_inv = _unit_tri_inverse_many([p["a"] for p in probs], eye)
    for p, t in zip(probs, t_inv):
        uw = _dot(t, p["rhs"])
        p["u"], p["w"] = uw[:, 0:G_HD], uw[:, G_HD:]
        p["v_new"] = [None] * nch
        p["o_inter"] = [None] * nch

    for ci in range(nch):
        rows = [slice(cc * CHUNK, (cc + 1) * CHUNK)
                for cc in [ci if p["di"] == 0 else nch - 1 - ci for p in probs]]
        s_bf = [p["s"].astype(BF16) for p in probs]
        v_new = [p["u"][r] - jnp.dot(p["w"][r].astype(BF16), sb, preferred_element_type=F32)
                 for p, r, sb in zip(probs, rows, s_bf)]
        upd = [_dot_tn(p["k_end"][r], vn) for p, r, vn in zip(probs, rows, v_new)]
        for p, r, sb, vn, up in zip(probs, rows, s_bf, v_new, upd):
            cc = r.start // CHUNK
            p["o_inter"][cc] = jnp.dot(p["qg"][r].astype(BF16), sb, preferred_element_type=F32)
            p["v_new"][cc] = vn
            p["s"] = p["s"] * p["carry"][r.start:r.start + 1, :] + up
    for p in probs:
        o = jnp.concatenate(p["o_inter"], axis=0) + _dot(p["qk"], jnp.concatenate(p["v_new"], axis=0))
        s_scr[p["di"] * G_HEADS + p["h"]] = p["s"]
        p["o_ref"][:, p["h"] * G_HD:(p["h"] + 1) * G_HD] = o


def _gdn(qkv, pg, gates_t, par_row, par_col, *, nct):
    scratch = [pltpu.VMEM((2 * G_HEADS, G_HD, G_HD), F32)]
    return _recurrent_call(_gdn_kernel, "gdn", qkv, 3 * G_W, pg, PG_GATE_BLK, gates_t,
                           [par_row, par_col], G_W, scratch, nct=nct)


def _post_kernel(x_ref, mod_ref, gain_ref, ya_ref, hf_ref, hb_ref, o_ref, of_ref, ob_ref, z_ref,
                 gs_ref, mn_ref, gn_ref, wb_ref, wo_ref, w1_ref, w2_ref, out_ref):
    hs = hf_ref[...] + hb_ref[...]
    og = o_ref[...]
    os_ = of_ref[...] + ob_ref[...]
    z = z_ref[...]
    ym, yg = [], []
    for h in range(M_HEADS):
        sl = slice(h * M_V, (h + 1) * M_V)
        ym.append(_rms(hs[:, sl], mn_ref[:, sl]) * _sigmoid(og[:, sl]))
    for h in range(G_HEADS):
        sl = slice(h * G_HD, (h + 1) * G_HD)
        zz = z[:, sl]
        yg.append(_rms(os_[:, sl], gn_ref[...]) * (zz * _sigmoid(zz)))
    ym = jnp.concatenate(ym, axis=-1)
    yg = jnp.concatenate(yg, axis=-1)
    y = (gs_ref[:, 0:D_MODEL].astype(F32) * _dot(ya_ref[...], wb_ref[0])
         + gs_ref[:, D_MODEL:2 * D_MODEL].astype(F32) * _dot(ym, wb_ref[1])
         + gs_ref[:, 2 * D_MODEL:].astype(F32) * _dot(yg, wb_ref[2]))
    y2 = _dot(y, wo_ref[...])
    x = x_ref[...] + mod_ref[2:3, :] * _rms(y2, gain_ref[1:2, :])

    h = (_rms(x, gain_ref[2:3, :]) * (1.0 + mod_ref[4:5, :]) + mod_ref[3:4, :]).astype(BF16)
    acc = jnp.zeros((x.shape[0], D_MODEL), F32)
    step = 1024
    for j in range(D_FF // step):
        a = jnp.maximum(jnp.dot(h, w1_ref[:, j * step:(j + 1) * step], preferred_element_type=F32), 0.0)
        acc = acc + jnp.dot((a * a).astype(BF16), w2_ref[j * step:(j + 1) * step, :], preferred_element_type=F32)
    out_ref[...] = x + mod_ref[5:6, :] * _rms(acc, gain_ref[3:4, :])


def _post(x_all, mod, gains, ya, hf, hb, pm, of, ob, pg, gsig, m_norm, g_norm, wb, wo, w1, w2, *, nct, t_off):
    bsz, t, _ = x_all.shape
    nt = t // TOKEN_TILE - t_off
    tile = lambda w, blk=0: pl.BlockSpec((None, TOKEN_TILE, w), lambda b, i: (b, i + t_off, blk))
    const = lambda shape: pl.BlockSpec(shape, lambda b, i: (0,) * len(shape), pipeline_mode=pl.Buffered(1))
    return pl.pallas_call(
        _post_kernel,
        grid=(bsz, nt),
        in_specs=[
            tile(D_MODEL),
            pl.BlockSpec((None, 6, D_MODEL), lambda b, i: (2 * b + (i + t_off >= nct).astype(jnp.int32), 0, 0)),
            const((4, D_MODEL)),
            tile(A_VW), tile(M_VW), tile(M_VW),
            tile(M_VW, (2 * M_QKW + M_VW) // M_VW),
            tile(G_W), tile(G_W),
            tile(G_W, 0),
            tile(3 * D_MODEL),
            const((1, M_VW)), const((1, G_HD)),
            const((3, BRANCH_W, D_MODEL)), const((D_MODEL, D_MODEL)),
            const((D_MODEL, D_FF)), const((D_FF, D_MODEL)),
        ],
        out_specs=pl.BlockSpec((None, TOKEN_TILE, D_MODEL), lambda b, i: (b, i, 0)),
        out_shape=jax.ShapeDtypeStruct((bsz, nt * TOKEN_TILE, D_MODEL), F32),
        compiler_params=_params(("parallel", "parallel")),
        name="post",
    )(x_all, mod, gains, ya, hf, hb, pm, of, ob, pg, gsig, m_norm, g_norm, wb, wo, w1, w2)


def _rope_tables(n_lat):
    rows = n_lat // GRID_W
    row = jnp.repeat(jnp.arange(rows, dtype=F32), GRID_W)
    col = jnp.tile(jnp.arange(GRID_W, dtype=F32), rows)
    half = A_HD // 2
    inv_freq = ROPE_BASE ** (-jnp.arange(0, half, 2, dtype=F32) / half)
    ang_r = row[:, None] * inv_freq
    ang_c = col[:, None] * inv_freq
    ang = jnp.concatenate([ang_r, ang_r, ang_c, ang_c] * 2, axis=-1)
    return jnp.cos(ang), jnp.sin(ang)


def _pad_cols(a, width):
    return jnp.pad(a, [(0, 0)] * (a.ndim - 1) + [(0, width - a.shape[-1])])


def _split_in_proj(w, b):
    def both(f):
        return f(w).astype(BF16), f(b[None, :])

    qk_idx = np.concatenate([np.concatenate([np.arange(_O_MQ + h * M_QK, _O_MQ + (h + 1) * M_QK),
                                             np.arange(_O_MK + h * M_QK, _O_MK + (h + 1) * M_QK)])
                             for h in range(M_HEADS)])
    attn = both(lambda a: a[:, _O_AQ:_O_MQ])
    mlstm = both(lambda a: jnp.concatenate(
        [a[:, qk_idx], a[:, _O_MV:_O_MG], _pad_cols(a[:, _O_MG:_O_GQ], LANES)], axis=-1))
    gdn_qkv = both(lambda a: a[:, _O_GQ:_O_GZ])
    gdn = both(lambda a: jnp.concatenate(
        [a[:, _O_GZ:_O_GG], _pad_cols(a[:, _O_GG:_O_MRG], LANES)], axis=-1))
    mrg = both(lambda a: a[:, _O_MRG:_O_END])
    return attn, mlstm, gdn_qkv, gdn, mrg


def kernel(x, c, ctx, c_ctx, w_ada, b_ada, norm_gains, w_in, b_in, a_lambda, a_norm, m_fbias, m_norm,
           g_conv, g_alog, g_dtbias, g_norm, w_branch, w_out, w_ff1, w_ff2):
    bsz, n_lat, _ = x.shape
    tc = ctx.shape[1]
    depth = w_ada.shape[0]
    assert tc % TOKEN_TILE == 0 and n_lat % TOKEN_TILE == 0 and TOKEN_TILE % CHUNK == 0
    nct = tc // TOKEN_TILE

    rows = -(-(bsz + 1) // 8) * 8
    cond = jnp.zeros((rows, D_MODEL), F32).at[:bsz].set(c).at[bsz].set(c_ctx)
    mod_all = _ada_all(cond, w_ada, b_ada)
    ctx_rows = jnp.broadcast_to(mod_all[:, bsz:bsz + 1], (depth, bsz, 6 * D_MODEL))
    mod_all = jnp.stack([ctx_rows, mod_all[:, :bsz]], axis=2).reshape(depth, 2 * bsz, 6, D_MODEL)

    rope = _rope_tables(n_lat)
    x_all = jnp.concatenate([ctx, x], axis=1)
    zeros4 = jnp.zeros((4,), F32)

    for i in range(depth):
        lam_init = 0.8 - 0.6 * math.exp(-0.3 * i)
        last = i == depth - 1
        mod = mod_all[i]
        gains = norm_gains[i]
        (wa, ba), (wm, bm), (wq, bq), (wg, bg), (wr, br) = _split_in_proj(w_in[i], b_in[i])
        g0 = gains[0:1]

        pa, pm, qkv, pg, gsig = _proj(x_all, mod, g0, (wa, wm, wq, wg, wr), (ba, bm, bq, bg, br),
                                      g_conv[i], rope, nct=nct)

        ya = _attention(pa, a_lambda[i], a_norm[i], nct=nct, tc=tc, lam_init=lam_init)

        fb = jnp.concatenate([zeros4, m_fbias[i, 0], zeros4, m_fbias[i, 1]])
        m_gt = jnp.swapaxes(pm[:, :, PM_GATE_BLK * LANES:PM_GATE_BLK * LANES + N_GATES], 1, 2)
        hf, hb = _mlstm(pm, m_gt, fb[None, :], fb[:, None], nct=nct)

        alog = jnp.concatenate([g_alog[i].reshape(-1), jnp.zeros((8,), F32)])
        dtb = jnp.concatenate([g_dtbias[i].reshape(-1), jnp.zeros((8,), F32)])
        par = jnp.stack([alog, dtb])
        g_gt = jnp.swapaxes(pg[:, :, PG_GATE_BLK * LANES:PG_GATE_BLK * LANES + N_GATES], 1, 2)
        of, ob = _gdn(qkv, pg, g_gt, par, par.T, nct=nct)

        t_off = nct if last else 0
        x_all = _post(x_all, mod, gains, ya, hf, hb, pm, of, ob, pg, gsig,
                      m_norm[i].reshape(1, M_VW), g_norm[i].reshape(1, G_HD),
                      w_branch[i].astype(BF16), w_out[i].astype(BF16),
                      w_ff1[i].astype(BF16), w_ff2[i].astype(BF16), nct=nct, t_off=t_off)
    return x_all
```

```python
import functools
import math

import jax
import jax.numpy as jnp
import numpy as np
from jax import lax
from jax.experimental import pallas as pl
from jax.experimental.pallas import tpu as pltpu

F32 = jnp.float32
BF16 = jnp.bfloat16

D_MODEL = 1024
GRID_W = 64
A_HEADS = 4
A_HD = 64
A_VD = 2 * A_HD
A_QK = A_HEADS * 2 * A_HD
A_VW = A_HEADS * A_VD
M_HEADS = 4
M_QK = 64
M_V = 128
M_QKW = M_HEADS * M_QK
M_VW = M_HEADS * M_V
G_HEADS = 4
G_HD = 128
G_W = G_HEADS * G_HD
CONV_K = 5
BRANCH_W = 512
D_FF = 4 * D_MODEL
ROPE_BASE = 10000.0
EPS = 1e-6
LOG2_E = math.log2(math.e)
N_GATES = 16

LANES = 128
TOKEN_TILE = 256
CHUNK = 64
VMEM_LIMIT = 56 * 1024 * 1024

_OFF = np.cumsum([0, A_QK, A_QK, A_VW, M_QKW, M_QKW, M_VW, M_VW, 4 * M_HEADS,
                  G_W, G_W, G_W, G_W, 4 * G_HEADS, 3 * D_MODEL])
(_O_AQ, _O_AK, _O_AV, _O_MQ, _O_MK, _O_MV, _O_MO, _O_MG,
 _O_GQ, _O_GK, _O_GV, _O_GZ, _O_GG, _O_MRG, _O_END) = [int(v) for v in _OFF]

PA_W = 3 * A_QK
PM_W = M_QKW * 2 + 2 * M_VW + LANES
PM_GATE_BLK = (M_QKW * 2 + 2 * M_VW) // LANES
PG_W = G_W + LANES
PG_GATE_BLK = G_W // LANES
HALO = 8


def _params(sem):
    return pltpu.CompilerParams(dimension_semantics=sem, vmem_limit_bytes=VMEM_LIMIT)


def _sigmoid(x):
    return 1.0 / (1.0 + jnp.exp(-x))


def _softplus(x):
    return jnp.maximum(x, 0.0) + jnp.log1p(jnp.exp(-jnp.abs(x)))


def _log_sigmoid(x):
    return -_softplus(-x)


def _rms(x, gain):
    return x * lax.rsqrt(jnp.mean(x * x, axis=-1, keepdims=True) + EPS) * gain


def _dot(a, b):
    return jnp.dot(a.astype(BF16), b.astype(BF16), preferred_element_type=F32)


def _dot_nt(a, b):
    return lax.dot_general(a.astype(BF16), b.astype(BF16), (((1,), (1,)), ((), ())),
                           preferred_element_type=F32)


def _dot_tn(a, b):
    return lax.dot_general(a.astype(BF16), b.astype(BF16), (((0,), (0,)), ((), ())),
                           preferred_element_type=F32)


def _dot_exact(a, b):
    return jnp.dot(a, b, precision=lax.Precision.HIGHEST, preferred_element_type=F32)


def _ada_kernel(c_ref, w_ref, b_ref, o_ref):
    c = c_ref[...]
    s = c * _sigmoid(c)
    o_ref[...] = _dot_exact(s, w_ref[...]) + b_ref[...]


def _ada_all(cond, w_ada, b_ada):
    depth = w_ada.shape[0]
    rows = cond.shape[0]
    n = w_ada.shape[2]
    tn = 1024
    return pl.pallas_call(
        _ada_kernel,
        grid=(depth, n // tn),
        in_specs=[pl.BlockSpec((rows, D_MODEL), lambda l, j: (0, 0)),
                  pl.BlockSpec((None, D_MODEL, tn), lambda l, j: (l, 0, j)),
                  pl.BlockSpec((None, 1, tn), lambda l, j: (l, 0, j))],
        out_specs=pl.BlockSpec((None, rows, tn), lambda l, j: (l, 0, j)),
        out_shape=jax.ShapeDtypeStruct((depth, rows, n), F32),
        compiler_params=_params(("parallel", "parallel")),
        name="ada",
    )(cond, w_ada, b_ada.reshape(depth, 1, n))


def _rot_half(x):
    lane = lax.broadcasted_iota(jnp.int32, x.shape, 1)
    first = (lane % 32) < 16
    return jnp.where(first, -pltpu.roll(x, LANES - 16, 1), pltpu.roll(x, 16, 1))


def _proj_kernel(x_ref, xp_ref, xn_ref, mod_ref, g_ref, wa_ref, wm_ref, wq_ref, wg_ref, wr_ref,
                 ba_ref, bm_ref, bq_ref, bg_ref, br_ref, cw_ref, cos_ref, sin_ref,
                 pa_ref, pm_ref, qkv_ref, pg_ref, gs_ref, *, nct, nt):
    tile = pl.program_id(1)
    x = x_ref[...]
    xe = jnp.concatenate([xp_ref[...], x, xn_ref[...]], axis=0)
    he = (_rms(xe, g_ref[...]) * (1.0 + mod_ref[1:2, :]) + mod_ref[0:1, :]).astype(BF16)
    h = he[HALO:HALO + TOKEN_TILE]

    def project(w_ref, b_ref, lhs=h):
        return jnp.dot(lhs, w_ref[...], preferred_element_type=F32) + b_ref[...]

    ext = project(wq_ref, bq_ref, he)
    n_ext = ext.shape[0]
    first = (tile == 0) | (tile == nct)
    last = (tile == nct - 1) | (tile == nt - 1)
    row = lax.broadcasted_iota(jnp.int32, (TOKEN_TILE, 1), 0)
    half = CONV_K // 2
    cw = cw_ref[...]
    conv = ext[HALO:HALO + TOKEN_TILE] * cw[half:half + 1, :]
    for d in range(-half, half + 1):
        if d == 0:
            continue
        shifted = pltpu.roll(ext, (-d) % n_ext, 0)[HALO:HALO + TOKEN_TILE]
        bad = ((row + d < 0) & first) | ((row + d >= TOKEN_TILE) & last)
        conv = conv + jnp.where(bad, 0.0, shifted) * cw[half + d:half + d + 1, :]
    y = conv * _sigmoid(conv)
    for j in range(3 * G_HEADS):
        sl = slice(j * G_HD, (j + 1) * G_HD)
        yh = y[:, sl]
        if j < 2 * G_HEADS:
            yh = yh * lax.rsqrt(jnp.sum(yh * yh, axis=-1, keepdims=True) + EPS)
        if j < G_HEADS:
            yh = yh * (G_HD ** -0.5)
        qkv_ref[:, sl] = yh

    acc = project(wa_ref, ba_ref)
    is_lat = tile >= nct
    cos = jnp.where(is_lat, cos_ref[...], 1.0)
    sin = jnp.where(is_lat, sin_ref[...], 0.0)
    for j in range(2 * A_QK // LANES):
        blk = acc[:, j * LANES:(j + 1) * LANES]
        r = blk * cos + _rot_half(blk) * sin
        if j < A_QK // LANES:
            r = r * (A_HD ** -0.5 * LOG2_E)
        pa_ref[:, j * LANES:(j + 1) * LANES] = r.astype(pa_ref.dtype)
    pa_ref[:, 2 * A_QK:] = acc[:, 2 * A_QK:].astype(pa_ref.dtype)
    pm_ref[...] = project(wm_ref, bm_ref)
    pg_ref[...] = project(wg_ref, bg_ref)
    gs_ref[...] = _sigmoid(project(wr_ref, br_ref)).astype(gs_ref.dtype)


def _proj(x_all, mod, gain, weights, biases, conv_w, rope, *, nct):
    bsz, t, _ = x_all.shape
    nt = t // TOKEN_TILE
    per_tile = TOKEN_TILE // HALO
    widths = [w.shape[1] for w in weights]
    const = lambda shape: pl.BlockSpec(shape, lambda i, j: (0, 0), pipeline_mode=pl.Buffered(1))
    rope_spec = pl.BlockSpec((TOKEN_TILE, LANES), lambda i, j: (jnp.maximum(j - nct, 0), 0))
    tile = lambda n: pl.BlockSpec((None, TOKEN_TILE, n), lambda i, j: (i, j, 0))
    prev_rows = pl.BlockSpec((None, HALO, D_MODEL), lambda i, j: (i, jnp.maximum(j * per_tile - 1, 0), 0))
    next_rows = pl.BlockSpec((None, HALO, D_MODEL),
                             lambda i, j: (i, jnp.minimum((j + 1) * per_tile, nt * per_tile - 1), 0))
    return pl.pallas_call(
        functools.partial(_proj_kernel, nct=nct, nt=nt),
        grid=(bsz, nt),
        in_specs=[tile(D_MODEL), prev_rows, next_rows,
                  pl.BlockSpec((None, 6, D_MODEL), lambda i, j: (2 * i + (j >= nct).astype(jnp.int32), 0, 0)),
                  const((1, D_MODEL))]
                 + [const((D_MODEL, n)) for n in widths] + [const((1, n)) for n in widths]
                 + [const(conv_w.shape), rope_spec, rope_spec],
        out_specs=[tile(n) for n in widths],
        out_shape=[jax.ShapeDtypeStruct((bsz, t, n), dt) for n, dt in zip(widths, (BF16, F32, F32, F32, BF16))],
        compiler_params=_params(("parallel", "parallel")),
        name="proj",
    )(x_all, x_all, x_all, mod, gain, *weights, *biases, conv_w, *rope)


def _attn_kernel(q_ref, k_ref, v_ref, lam_ref, an_ref, o_ref, *, nct, tc, t_all, lam_init):
    qi = pl.program_id(2)
    lv = lam_ref[...].astype(F32)
    lam = (jnp.exp(jnp.sum(lv[0:1] * lv[1:2], axis=-1, keepdims=True))
           - jnp.exp(jnp.sum(lv[2:3] * lv[3:4], axis=-1, keepdims=True)) + lam_init)

    def attend(n_keys):
        for hd in range(2):
            q = q_ref[:, hd * LANES:(hd + 1) * LANES]
            lane = lax.broadcasted_iota(jnp.int32, q.shape, 1)
            zero = jnp.zeros_like(q)
            qm = (jnp.where(lane < A_HD, q, zero), jnp.where(lane >= A_HD, q, zero))
            pv = []
            for mp in range(2):
                s = lax.dot_general(qm[mp], k_ref[0:n_keys, hd * LANES:(hd + 1) * LANES], (((1,), (1,)), ((), ())),
                                    preferred_element_type=F32)
                p = jnp.exp2(s - jnp.max(s, axis=-1, keepdims=True))
                l = jnp.sum(p, axis=-1, keepdims=True)
                pv.append(jnp.dot(p.astype(BF16), v_ref[0:n_keys, hd * LANES:(hd + 1) * LANES],
                                  preferred_element_type=F32) / l)
            o = pv[0] - lam * pv[1]
            o_ref[:, hd * LANES:(hd + 1) * LANES] = _rms(o, an_ref[...]) * (1.0 - lam_init)

    @pl.when(qi < nct)
    def _():
        attend(tc)

    @pl.when(qi >= nct)
    def _():
        attend(t_all)


def _attention(pa, a_lambda, a_norm, *, nct, tc, lam_init):
    bsz, t, _ = pa.shape
    nt = t // TOKEN_TILE
    w2 = 2 * LANES
    return pl.pallas_call(
        functools.partial(_attn_kernel, nct=nct, tc=tc, t_all=t, lam_init=lam_init),
        grid=(bsz, A_HEADS // 2, nt),
        in_specs=[
            pl.BlockSpec((None, TOKEN_TILE, w2), lambda b, h, i: (b, i, h)),
            pl.BlockSpec((None, t, w2), lambda b, h, i: (b, 0, A_QK // w2 + h)),
            pl.BlockSpec((None, t, w2), lambda b, h, i: (b, 0, 2 * A_QK // w2 + h)),
            pl.BlockSpec((4, A_HD), lambda b, h, i: (0, 0)),
            pl.BlockSpec((1, A_VD), lambda b, h, i: (0, 0)),
        ],
        out_specs=pl.BlockSpec((None, TOKEN_TILE, w2), lambda b, h, i: (b, i, h)),
        out_shape=jax.ShapeDtypeStruct((bsz, t, A_VW), F32),
        compiler_params=_params(("parallel", "parallel", "arbitrary")),
        name="diff_attn",
    )(pa, pa, pa, a_lambda, a_norm.reshape(1, A_VD))


def _bwd_tile(s, nct, nt):
    return jnp.where(s < nct, nct - 1 - s, nt - 1 - (s - nct))


def _split2(x):
    hi = x.astype(BF16)
    return hi, (x - hi.astype(F32)).astype(BF16)


def _split3(x):
    hi = x.astype(BF16)
    r = x - hi.astype(F32)
    mid = r.astype(BF16)
    return hi, mid, (r - mid.astype(F32)).astype(BF16)


def _dot_f32ish(a2, b2):
    (ah, al), (bh, bl) = a2, b2
    return (jnp.dot(ah, bl, preferred_element_type=F32) + jnp.dot(al, bh, preferred_element_type=F32)
            + jnp.dot(ah, bh, preferred_element_type=F32))


def _sel_dot(sel, x):
    hi, mid, lo = _split3(x)
    return (jnp.dot(sel, lo, preferred_element_type=F32) + jnp.dot(sel, mid, preferred_element_type=F32)
            + jnp.dot(sel, hi, preferred_element_type=F32))


def _dot_sel(x, sel):
    hi, mid, lo = _split3(x)
    return (jnp.dot(lo, sel, preferred_element_type=F32) + jnp.dot(mid, sel, preferred_element_type=F32)
            + jnp.dot(hi, sel, preferred_element_type=F32))


def _recurrent_call(kernel_fn, name, qkv, qkv_w, gate_src, gate_blk, gates_t, small, out_w, scratch, *, nct):
    bsz, t, _ = qkv.shape
    nt = t // TOKEN_TILE
    fwd = lambda b, s: (b, s, 0)
    bwd = lambda b, s: (b, _bwd_tile(s, nct, nt), 0)
    fwd_g = lambda b, s: (b, s, gate_blk)
    bwd_g = lambda b, s: (b, _bwd_tile(s, nct, nt), gate_blk)
    fwd_t = lambda b, s: (b, 0, s)
    bwd_t = lambda b, s: (b, 0, _bwd_tile(s, nct, nt))
    out = jax.ShapeDtypeStruct((bsz, t, out_w), F32)
    return pl.pallas_call(
        kernel_fn,
        grid=(bsz, nt),
        in_specs=[
            pl.BlockSpec((None, TOKEN_TILE, qkv_w), fwd),
            pl.BlockSpec((None, TOKEN_TILE, LANES), fwd_g),
            pl.BlockSpec((None, N_GATES, TOKEN_TILE), fwd_t),
            pl.BlockSpec((None, TOKEN_TILE, qkv_w), bwd),
            pl.BlockSpec((None, TOKEN_TILE, LANES), bwd_g),
            pl.BlockSpec((None, N_GATES, TOKEN_TILE), bwd_t),
        ] + [pl.BlockSpec(a.shape, lambda b, s: (0, 0)) for a in small],
        out_specs=[pl.BlockSpec((None, TOKEN_TILE, out_w), fwd),
                   pl.BlockSpec((None, TOKEN_TILE, out_w), bwd)],
        out_shape=[out, out],
        scratch_shapes=scratch,
        compiler_params=_params(("parallel", "arbitrary")),
        name=name,
    )(qkv, gate_src, gates_t, qkv, gate_src, gates_t, *small)


def _mlstm_kernel(qkv_f, g_f, gt_f, qkv_b, g_b, gt_b, fbr_ref, fbc_ref, hf_ref, hb_ref,
                  c_scr, m_scr):
    @pl.when(pl.program_id(1) == 0)
    def _():
        c_scr[...] = jnp.zeros_like(c_scr)
        m_scr[...] = jnp.zeros_like(m_scr)

    n = TOKEN_TILE
    row = lax.broadcasted_iota(jnp.int32, (n, n), 0)
    col = lax.broadcasted_iota(jnp.int32, (n, n), 1)
    lower, upper = col <= row, col >= row
    lower_s, upper_s = lower.astype(BF16), upper.astype(BF16)
    fbr = fbr_ref[...]
    fbc = fbc_ref[...]
    dirs = ((qkv_f, g_f, gt_f, hf_ref, lower, lower_s, upper_s),
            (qkv_b, g_b, gt_b, hb_ref, upper, upper_s, lower_s))

    probs = []
    for di, (qkv, g_ref, gt_ref, h_ref, incl, csum, csum_t) in enumerate(dirs):
        g = g_ref[:, 0:N_GATES]
        gt = gt_ref[...]
        ls = _log_sigmoid(g + fbr)
        lst = _log_sigmoid(gt + fbc)
        bc_all = _sel_dot(csum, ls)
        br_all = _dot_sel(lst, csum_t)
        tot = jnp.sum(ls, axis=0, keepdims=True)
        for h in range(M_HEADS):
            ic = di * 8 + h
            fc = di * 8 + 4 + h
            base = h * 2 * M_QK
            st = di * M_HEADS + h
            probs.append(dict(
                st=st, h=h, h_ref=h_ref, incl=incl,
                q=qkv[:, base:base + M_QK],
                k=qkv[:, base + M_QK:base + 2 * M_QK] * (M_QK ** -0.5),
                v=qkv[:, 2 * M_QKW + h * M_V:2 * M_QKW + (h + 1) * M_V],
                i_c=g[:, ic:ic + 1], b_c=bc_all[:, fc:fc + 1], i_r=gt[ic:ic + 1, :],
                b_r=br_all[fc:fc + 1, :], btot=tot[:, fc:fc + 1],
                c=c_scr[st], m=m_scr[st]))
    ones = jnp.ones((n, M_V), BF16)
    for p in probs:
        p["v2"] = jnp.concatenate([p["v"].astype(BF16), ones], axis=-1)
        p["qk"] = _dot_nt(p["q"], p["k"])
        p["qc"] = _dot(p["q"], p["c"])
    for p in probs:
        d = jnp.where(p["incl"], p["b_c"] - p["b_r"] + p["i_r"], -jnp.inf)
        inter = p["b_c"] + p["m"]
        p["m_row"] = jnp.maximum(inter, jnp.max(d, axis=-1, keepdims=True))
        p["e_inter"] = jnp.exp(inter - p["m_row"])
        p["s"] = p["qk"] * jnp.exp(d - p["m_row"])
    for p in probs:
        a_r = p["btot"] - p["b_r"] + p["i_r"]
        a_c = p["btot"] - p["b_c"] + p["i_c"]
        m_new = jnp.maximum(p["btot"] + p["m"], jnp.max(a_r, axis=-1, keepdims=True))
        p["carry_w"] = jnp.exp(p["btot"] + p["m"] - m_new)
        p["kw"] = p["k"] * jnp.exp(a_c - m_new)
        m_scr[p["st"]] = m_new
    for p in probs:
        both = jnp.dot(p["s"].astype(BF16), p["v2"], preferred_element_type=F32) + p["e_inter"] * p["qc"]
        hh = both[:, 0:M_V] / jnp.maximum(jnp.abs(both[:, M_V:]), jnp.exp(-p["m_row"]))
        p["h_ref"][:, p["h"] * M_V:(p["h"] + 1) * M_V] = hh
    for p in probs:
        c_scr[p["st"]] = p["carry_w"] * p["c"] + _dot_tn(p["kw"], p["v2"])


def _mlstm(pm, gates_t, fb_row, fb_col, *, nct):
    scratch = [pltpu.VMEM((2 * M_HEADS, M_QK, 2 * M_V), F32),
               pltpu.VMEM((2 * M_HEADS, 1, 1), F32)]
    return _recurrent_call(_mlstm_kernel, "mlstm", pm, 2 * M_QKW + M_VW, pm, PM_GATE_BLK, gates_t,
                           [fb_row, fb_col], M_VW, scratch, nct=nct)


def _unit_tri_inverse_many(a_list, eye):
    n_iter = int(math.log2(CHUNK)) - 1
    xs = [(eye - a).astype(BF16) for a in a_list]
    ps = [a.astype(BF16) for a in a_list]
    for it in range(n_iter):
        ps = [jnp.dot(p, p, preferred_element_type=F32).astype(BF16) for p in ps]
        xs = [x.astype(F32) + jnp.dot(x, p, preferred_element_type=F32) for x, p in zip(xs, ps)]
        if it < n_iter - 1:
            xs = [x.astype(BF16) for x in xs]
    out = []
    for a, x in zip(a_list, xs):
        x2 = _split2(x)
        r = (eye - x) - _dot_f32ish(_split2(a), x2)
        out.append(x + jnp.dot(x2[0], r.astype(BF16), preferred_element_type=F32))
    return out


def _gdn_kernel(qkv_f, g_f, gt_f, qkv_b, g_b, gt_b, par_r, par_c, of_ref, ob_ref, s_scr):
    @pl.when(pl.program_id(1) == 0)
    def _():
        s_scr[...] = jnp.zeros_like(s_scr)

    n = TOKEN_TILE
    nch = n // CHUNK
    row = lax.broadcasted_iota(jnp.int32, (n, n), 0)
    col = lax.broadcasted_iota(jnp.int32, (n, n), 1)
    same = (row // CHUNK) == (col // CHUNK)
    lower, upper = same & (col <= row), same & (col >= row)
    lower_x, upper_x = same & (col < row), same & (col > row)
    lower_s, upper_s, same_s = lower.astype(BF16), upper.astype(BF16), same.astype(BF16)
    eye = jnp.where(row == col, 1.0, 0.0)
    alog_r, dtb_r = par_r[0:1, :], par_r[1:2, :]
    alog_c, dtb_c = par_c[:, 0:1], par_c[:, 1:2]
    col_r = lax.broadcasted_iota(jnp.int32, (n, N_GATES), 1)
    row_c = lax.broadcasted_iota(jnp.int32, (N_GATES, n), 0)
    dirs = ((qkv_f, g_f, gt_f, of_ref, lower, lower_x, lower_s, upper_s),
            (qkv_b, g_b, gt_b, ob_ref, upper, upper_x, upper_s, lower_s))

    probs = []
    for di, (qkv, g_ref, gt_ref, o_ref, incl, strict, csum, csum_t) in enumerate(dirs):
        g = g_ref[:, 0:N_GATES]
        gt = gt_ref[...]
        val = jnp.where(col_r < 8, -jnp.exp(alog_r) * _softplus(g + dtb_r), _sigmoid(g))
        val_t = jnp.where(row_c < 8, -jnp.exp(alog_c) * _softplus(gt + dtb_c), _sigmoid(gt))
        gc_all = _sel_dot(csum, val)
        gr_all = _dot_sel(val_t, csum_t)
        ge_all = _sel_dot(same_s, val)
        for h in range(G_HEADS):
            ac = di * 4 + h
            bc = 8 + di * 4 + h
            q = qkv[:, h * G_HD:(h + 1) * G_HD]
            k = qkv[:, G_W + h * G_HD:G_W + (h + 1) * G_HD]
            v = qkv[:, 2 * G_W + h * G_HD:2 * G_W + (h + 1) * G_HD]
            gc_c, gc_r = gc_all[:, ac:ac + 1], gr_all[ac:ac + 1, :]
            beta_c, ge_c = val[:, bc:bc + 1], ge_all[:, ac:ac + 1]
            decay = jnp.where(incl, jnp.exp(jnp.where(incl, gc_c - gc_r, 0.0)), 0.0)
            kb = k * beta_c
            eg = jnp.exp(gc_c)
            probs.append(dict(
                di=di, h=h, o_ref=o_ref,
                a=jnp.where(strict, _dot_nt(kb, k) * decay, 0.0),
                rhs=jnp.concatenate([v * beta_c, kb * eg], axis=-1),
                k_end=k * jnp.exp(ge_c - gc_c),
                qk=jnp.where(incl, _dot_nt(q, k) * decay, 0.0),
                qg=q * eg,
                carry=jnp.exp(ge_c),
                s=s_scr[di * G_HEADS + h]))
    t_inv = _unit_tri_inverse_many([p["a"] for p in probs], eye)
    for p, t in zip(probs, t_inv):
        uw = _dot(t, p["rhs"])
        p["u"], p["w"] = uw[:, 0:G_HD], uw[:, G_HD:]
        p["v_new"] = [None] * nch
        p["o_inter"] = [None] * nch

    for ci in range(nch):
        rows = [slice(cc * CHUNK, (cc + 1) * CHUNK)
                for cc in [ci if p["di"] == 0 else nch - 1 - ci for p in probs]]
        s_bf = [p["s"].astype(BF16) for p in probs]
        v_new = [p["u"][r] - jnp.dot(p["w"][r].astype(BF16), sb, preferred_element_type=F32)
                 for p, r, sb in zip(probs, rows, s_bf)]
        upd = [_dot_tn(p["k_end"][r], vn) for p, r, vn in zip(probs, rows, v_new)]
        for p, r, sb, vn, up in zip(probs, rows, s_bf, v_new, upd):
            cc = r.start // CHUNK
            p["o_inter"][cc] = jnp.dot(p["qg"][r].astype(BF16), sb, preferred_element_type=F32)
            p["v_new"][cc] = vn
            p["s"] = p["s"] * p["carry"][r.start:r.start + 1, :] + up
    for p in probs:
        o = jnp.concatenate(p["o_inter"], axis=0) + _dot(p["qk"], jnp.concatenate(p["v_new"], axis=0))
        s_scr[p["di"] * G_HEADS + p["h"]] = p["s"]
        p["o_ref"][:, p["h"] * G_HD:(p["h"] + 1) * G_HD] = o


def _gdn(qkv, pg, gates_t, par_row, par_col, *, nct):
    scratch = [pltpu.VMEM((2 * G_HEADS, G_HD, G_HD), F32)]
    return _recurrent_call(_gdn_kernel, "gdn", qkv, 3 * G_W, pg, PG_GATE_BLK, gates_t,
                           [par_row, par_col], G_W, scratch, nct=nct)


def _post_kernel(x_ref, mod_ref, gain_ref, ya_ref, hf_ref, hb_ref, o_ref, of_ref, ob_ref, z_ref,
                 gs_ref, mn_ref, gn_ref, wb_ref, wo_ref, w1_ref, w2_ref, out_ref):
    hs = hf_ref[...] + hb_ref[...]
    og = o_ref[...]
    os_ = of_ref[...] + ob_ref[...]
    z = z_ref[...]
    ym, yg = [], []
    for h in range(M_HEADS):
        sl = slice(h * M_V, (h + 1) * M_V)
        ym.append(_rms(hs[:, sl], mn_ref[:, sl]) * _sigmoid(og[:, sl]))
    for h in range(G_HEADS):
        sl = slice(h * G_HD, (h + 1) * G_HD)
        zz = z[:, sl]
        yg.append(_rms(os_[:, sl], gn_ref[...]) * (zz * _sigmoid(zz)))
    ym = jnp.concatenate(ym, axis=-1)
    yg = jnp.concatenate(yg, axis=-1)
    y = (gs_ref[:, 0:D_MODEL].astype(F32) * _dot(ya_ref[...], wb_ref[0])
         + gs_ref[:, D_MODEL:2 * D_MODEL].astype(F32) * _dot(ym, wb_ref[1])
         + gs_ref[:, 2 * D_MODEL:].astype(F32) * _dot(yg, wb_ref[2]))
    y2 = _dot(y, wo_ref[...])
    x = x_ref[...] + mod_ref[2:3, :] * _rms(y2, gain_ref[1:2, :])

    h = (_rms(x, gain_ref[2:3, :]) * (1.0 + mod_ref[4:5, :]) + mod_ref[3:4, :]).astype(BF16)
    acc = jnp.zeros((x.shape[0], D_MODEL), F32)
    step = 1024
    for j in range(D_FF // step):
        a = jnp.maximum(jnp.dot(h, w1_ref[:, j * step:(j + 1) * step], preferred_element_type=F32), 0.0)
        acc = acc + jnp.dot((a * a).astype(BF16), w2_ref[j * step:(j + 1) * step, :], preferred_element_type=F32)
    out_ref[...] = x + mod_ref[5:6, :] * _rms(acc, gain_ref[3:4, :])


def _post(x_all, mod, gains, ya, hf, hb, pm, of, ob, pg, gsig, m_norm, g_norm, wb, wo, w1, w2, *, nct, t_off):
    bsz, t, _ = x_all.shape
    nt = t // TOKEN_TILE - t_off
    tile = lambda w, blk=0: pl.BlockSpec((None, TOKEN_TILE, w), lambda b, i: (b, i + t_off, blk))
    const = lambda shape: pl.BlockSpec(shape, lambda b, i: (0,) * len(shape), pipeline_mode=pl.Buffered(1))
    return pl.pallas_call(
        _post_kernel,
        grid=(bsz, nt),
        in_specs=[
            tile(D_MODEL),
            pl.BlockSpec((None, 6, D_MODEL), lambda b, i: (2 * b + (i + t_off >= nct).astype(jnp.int32), 0, 0)),
            const((4, D_MODEL)),
            tile(A_VW), tile(M_VW), tile(M_VW),
            tile(M_VW, (2 * M_QKW + M_VW) // M_VW),
            tile(G_W), tile(G_W),
            tile(G_W, 0),
            tile(3 * D_MODEL),
            const((1, M_VW)), const((1, G_HD)),
            const((3, BRANCH_W, D_MODEL)), const((D_MODEL, D_MODEL)),
            const((D_MODEL, D_FF)), const((D_FF, D_MODEL)),
        ],
        out_specs=pl.BlockSpec((None, TOKEN_TILE, D_MODEL), lambda b, i: (b, i, 0)),
        out_shape=jax.ShapeDtypeStruct((bsz, nt * TOKEN_TILE, D_MODEL), F32),
        compiler_params=_params(("parallel", "parallel")),
        name="post",
    )(x_all, mod, gains, ya, hf, hb, pm, of, ob, pg, gsig, m_norm, g_norm, wb, wo, w1, w2)


def _rope_tables(n_lat):
    rows = n_lat // GRID_W
    row = jnp.repeat(jnp.arange(rows, dtype=F32), GRID_W)
    col = jnp.tile(jnp.arange(GRID_W, dtype=F32), rows)
    half = A_HD // 2
    inv_freq = ROPE_BASE ** (-jnp.arange(0, half, 2, dtype=F32) / half)
    ang_r = row[:, None] * inv_freq
    ang_c = col[:, None] * inv_freq
    ang = jnp.concatenate([ang_r, ang_r, ang_c, ang_c] * 2, axis=-1)
    return jnp.cos(ang), jnp.sin(ang)


def _pad_cols(a, width):
    return jnp.pad(a, [(0, 0)] * (a.ndim - 1) + [(0, width - a.shape[-1])])


def _split_in_proj(w, b):
    def both(f):
        return f(w).astype(BF16), f(b[None, :])

    qk_idx = np.concatenate([np.concatenate([np.arange(_O_MQ + h * M_QK, _O_MQ + (h + 1) * M_QK),
                                             np.arange(_O_MK + h * M_QK, _O_MK + (h + 1) * M_QK)])
                             for h in range(M_HEADS)])
    attn = both(lambda a: a[:, _O_AQ:_O_MQ])
    mlstm = both(lambda a: jnp.concatenate(
        [a[:, qk_idx], a[:, _O_MV:_O_MG], _pad_cols(a[:, _O_MG:_O_GQ], LANES)], axis=-1))
    gdn_qkv = both(lambda a: a[:, _O_GQ:_O_GZ])
    gdn = both(lambda a: jnp.concatenate(
        [a[:, _O_GZ:_O_GG], _pad_cols(a[:, _O_GG:_O_MRG], LANES)], axis=-1))
    mrg = both(lambda a: a[:, _O_MRG:_O_END])
    return attn, mlstm, gdn_qkv, gdn, mrg


def kernel(x, c, ctx, c_ctx, w_ada, b_ada, norm_gains, w_in, b_in, a_lambda, a_norm, m_fbias, m_norm,
           g_conv, g_alog, g_dtbias, g_norm, w_branch, w_out, w_ff1, w_ff2):
    bsz, n_lat, _ = x.shape
    tc = ctx.shape[1]
    depth = w_ada.shape[0]
    assert tc % TOKEN_TILE == 0 and n_lat % TOKEN_TILE == 0 and TOKEN_TILE % CHUNK == 0
    nct = tc // TOKEN_TILE

    rows = -(-(bsz + 1) // 8) * 8
    cond = jnp.zeros((rows, D_MODEL), F32).at[:bsz].set(c).at[bsz].set(c_ctx)
    mod_all = _ada_all(cond, w_ada, b_ada)
    ctx_rows = jnp.broadcast_to(mod_all[:, bsz:bsz + 1], (depth, bsz, 6 * D_MODEL))
    mod_all = jnp.stack([ctx_rows, mod_all[:, :bsz]], axis=2).reshape(depth, 2 * bsz, 6, D_MODEL)

    rope = _rope_tables(n_lat)
    x_all = jnp.concatenate([ctx, x], axis=1)
    zeros4 = jnp.zeros((4,), F32)

    for i in range(depth):
        lam_init = 0.8 - 0.6 * math.exp(-0.3 * i)
        last = i == depth - 1
        mod = mod_all[i]
        gains = norm_gains[i]
        (wa, ba), (wm, bm), (wq, bq), (wg, bg), (wr, br) = _split_in_proj(w_in[i], b_in[i])
        g0 = gains[0:1]

        pa, pm, qkv, pg, gsig = _proj(x_all, mod, g0, (wa, wm, wq, wg, wr), (ba, bm, bq, bg, br),
                                      g_conv[i], rope, nct=nct)

        ya = _attention(pa, a_lambda[i], a_norm[i], nct=nct, tc=tc, lam_init=lam_init)

        fb = jnp.concatenate([zeros4, m_fbias[i, 0], zeros4, m_fbias[i, 1]])
        m_gt = jnp.swapaxes(pm[:, :, PM_GATE_BLK * LANES:PM_GATE_BLK * LANES + N_GATES], 1, 2)
        hf, hb = _mlstm(pm, m_gt, fb[None, :], fb[:, None], nct=nct)

        alog = jnp.concatenate([g_alog[i].reshape(-1), jnp.zeros((8,), F32)])
        dtb = jnp.concatenate([g_dtbias[i].reshape(-1), jnp.zeros((8,), F32)])
        par = jnp.stack([alog, dtb])
        g_gt = jnp.swapaxes(pg[:, :, PG_GATE_BLK * LANES:PG_GATE_BLK * LANES + N_GATES], 1, 2)
        of, ob = _gdn(qkv, pg, g_gt, par, par.T, nct=nct)

        t_off = nct if last else 0
        x_all = _post(x_all, mod, gains, ya, hf, hb, pm, of, ob, pg, gsig,
                      m_norm[i].reshape(1, M_VW), g_norm[i].reshape(1, G_HD),
                      w_branch[i].astype(BF16), w_out[i].astype(BF16),
                      w_ff1[i].astype(BF16), w_ff2[i].astype(BF16), nct=nct, t_off=t_off)
    return x_all
```
